```python
import jax, jax.numpy as jnp
from jax import lax
import numpy as np

D_MODEL = 1024
BATCH = 8
SEQ = 2048
DEPTH = 2
DEC_BATCH = 128
DEC_SEQ = 8
PAST_LEN = 2048
PAGE_SIZE = 128

N_MIXERS = 2
N_ATTN_LAYERS = (DEPTH + 1) // 2
N_RET_LAYERS = DEPTH // 2
ATTN_HEADS = 16
ATTN_HEAD_DIM = 64
ATTN_WIDTH = ATTN_HEADS * ATTN_HEAD_DIM
IDX_HEADS = 8
IDX_DIM = 64
IDX_TOPK_MAX = 256
Q_BLOCK = 128
ROPE_THETA = 10000.0
ATTN_IN = 3 * ATTN_WIDTH + IDX_HEADS * IDX_DIM + IDX_DIM + IDX_HEADS
RET_HEADS = 4
RET_QK_DIM = D_MODEL // RET_HEADS
RET_V_DIM = 2 * D_MODEL // RET_HEADS
RET_CHUNK = 128
RET_IN = 2 * RET_HEADS * RET_QK_DIM + 2 * RET_HEADS * RET_V_DIM
N_EXPERTS = 16
N_GROUPS = 4
EXPERTS_PER_GROUP = N_EXPERTS // N_GROUPS
TOPK_EXPERTS = 2
EXPERT_DIM = D_MODEL // 2
DEEPNORM_ALPHA = (2 * DEPTH) ** 0.25
DEEPNORM_BETA = (8 * DEPTH) ** -0.25
LN_EPS = 1e-5

kernel_name = "hybrid_dsa_retention_moe_step"

F32 = jnp.float32


def layer_norm(x, g, b):
    xf = x.astype(F32)
    mu = jnp.mean(xf, -1, keepdims=True)
    var = jnp.mean(jnp.square(xf - mu), -1, keepdims=True)
    return ((xf - mu) * lax.rsqrt(var + LN_EPS) * g.astype(F32) + b.astype(F32)).astype(x.dtype)


def rope(x, pos):
    d = x.shape[-1]
    inv = ROPE_THETA ** (-jnp.arange(0, d, 2, dtype=F32) / d)
    ang = pos.astype(F32)[:, None] * inv[None, :]
    cos = jnp.cos(ang)[:, None, :].astype(x.dtype)
    sin = jnp.sin(ang)[:, None, :].astype(x.dtype)
    x1, x2 = x[..., : d // 2], x[..., d // 2:]
    return jnp.concatenate([x1 * cos - x2 * sin, x1 * sin + x2 * cos], -1)


def dsa_project(x, w_in, pos):
    B, T, _ = x.shape
    W = ATTN_WIDTH
    cuts = [W, 2 * W, 3 * W, 3 * W + IDX_HEADS * IDX_DIM, 3 * W + IDX_HEADS * IDX_DIM + IDX_DIM]
    q, k, v, qi, ki, wi = jnp.split(x @ w_in, cuts, axis=-1)
    q = rope(q.reshape(B, T, ATTN_HEADS, ATTN_HEAD_DIM), pos)
    k = rope(k.reshape(B, T, ATTN_HEADS, ATTN_HEAD_DIM), pos)
    v = v.reshape(B, T, ATTN_HEADS, ATTN_HEAD_DIM)
    qi = rope(qi.reshape(B, T, IDX_HEADS, IDX_DIM), pos)
    ki = rope(ki.reshape(B, T, 1, IDX_DIM), pos)[:, :, 0, :]
    wi = wi * IDX_HEADS ** -0.5
    return q, k, v, qi, ki, wi


def index_select(qi, wi, ki, qpos, kpos, topk):
    s = jnp.einsum('qhd,sd->qhs', qi.astype(F32), ki.astype(F32)) * IDX_DIM ** -0.5
    score = jnp.einsum('qhs,qh->qs', jax.nn.relu(s), wi.astype(F32))
    score = jnp.where(kpos[None, :] <= qpos[:, None], score, -jnp.inf)
    vals, idx = lax.top_k(score, topk)
    return idx, jnp.isfinite(vals)


def sparse_attend(q, k_sel, v_sel, valid):
    s = jnp.einsum('qhd,qshd->qhs', q.astype(F32), k_sel.astype(F32)) * ATTN_HEAD_DIM ** -0.5
    p = jax.nn.softmax(jnp.where(valid[:, None, :], s, -jnp.inf), axis=-1)
    return jnp.einsum('qhs,qshd->qhd', p, v_sel.astype(F32)).astype(q.dtype)


def dsa_prompt(x, w_in, w_out, pos, topk):
    B, T, _ = x.shape
    q, k, v, qi, ki, wi = dsa_project(x, w_in, pos)
    nb = T // Q_BLOCK

    def block(args):
        q_b, qi_b, wi_b, qpos_b, b = args
        idx, valid = index_select(qi_b, wi_b, ki[b], qpos_b, pos, topk)
        return sparse_attend(q_b, k[b, idx], v[b, idx], valid)

    xs = (q.reshape(B * nb, Q_BLOCK, ATTN_HEADS, ATTN_HEAD_DIM),
          qi.reshape(B * nb, Q_BLOCK, IDX_HEADS, IDX_DIM),
          wi.reshape(B * nb, Q_BLOCK, IDX_HEADS),
          jnp.tile(pos.reshape(nb, Q_BLOCK), (B, 1)),
          jnp.repeat(jnp.arange(B, dtype=jnp.int32), nb))
    o = lax.map(block, xs).reshape(B, T, ATTN_WIDTH)
    return o @ w_out, k, v, ki


def dsa_sample(x, w_in, w_out, cache_k, cache_v, cache_ik, page_table, pos, topk):
    Bs, Ts, _ = x.shape
    q, k, v, qi, ki, wi = dsa_project(x, w_in, pos)
    past = page_table.shape[1] * PAGE_SIZE
    kpos = jnp.arange(past + Ts, dtype=jnp.int32)

    def seq(args):
        q_b, qi_b, wi_b, k_b, v_b, ki_b, pt = args
        ki_all = jnp.concatenate([cache_ik[pt].reshape(past, IDX_DIM), ki_b], 0)
        idx, valid = index_select(qi_b, wi_b, ki_all, pos, kpos, topk)
        in_past = (idx < past)[:, :, None, None]
        ip = jnp.minimum(idx, past - 1)
        phys, off = pt[ip // PAGE_SIZE], ip % PAGE_SIZE
        inew = jnp.clip(idx - past, 0, Ts - 1)
        k_sel = jnp.where(in_past, cache_k[phys, off], k_b[inew])
        v_sel = jnp.where(in_past, cache_v[phys, off], v_b[inew])
        return sparse_attend(q_b, k_sel, v_sel, valid)

    o = lax.map(seq, (q, qi, wi, k, v, ki, page_table)).reshape(Bs, Ts, ATTN_WIDTH)
    return o @ w_out, k, v, ki


def retention_log_decay():
    return jnp.log(1.0 - 2.0 ** (-5.0 - jnp.arange(RET_HEADS, dtype=F32)))


def retention_project(x, w_in, pos):
    B, T, _ = x.shape
    hq, hv = RET_HEADS * RET_QK_DIM, RET_HEADS * RET_V_DIM
    q, k, v, g = jnp.split(x @ w_in, [hq, 2 * hq, 2 * hq + hv], axis=-1)
    q = rope(q.reshape(B, T, RET_HEADS, RET_QK_DIM), pos)
    k = rope(k.reshape(B, T, RET_HEADS, RET_QK_DIM), pos) * RET_QK_DIM ** -0.5
    v = v.reshape(B, T, RET_HEADS, RET_V_DIM)
    return q, k, v, g


def retention_chunk(state, q, k, v):
    C = q.shape[1]
    lg = retention_log_decay()
    i = jnp.arange(C, dtype=F32)
    diff = i[:, None] - i[None, :]
    dmask = jnp.where(diff[None] >= 0, jnp.exp(jnp.maximum(diff, 0.0)[None] * lg[:, None, None]), 0.0).astype(q.dtype)
    cross_decay = jnp.exp((i + 1.0)[:, None] * lg[None, :]).astype(q.dtype)
    k_decay = jnp.exp((C - 1.0 - i)[:, None] * lg[None, :]).astype(q.dtype)
    chunk_decay = jnp.exp(C * lg).astype(q.dtype)
    inner = jnp.einsum('bihd,bjhd->bhij', q, k) * dmask[None]
    o = jnp.einsum('bhij,bjhv->bihv', inner, v) + \
        jnp.einsum('bihd,bhdv->bihv', q, state) * cross_decay[None, :, :, None]
    new_state = chunk_decay[None, :, None, None] * state + \
        jnp.einsum('bjhd,bjhv->bhdv', k * k_decay[None, :, :, None], v)
    return new_state.astype(state.dtype), o


def retention_output(o, g, gain, w_out):
    B, T = o.shape[:2]
    of = o.astype(F32)
    mu = jnp.mean(of, -1, keepdims=True)
    var = jnp.mean(jnp.square(of - mu), -1, keepdims=True)
    on = ((of - mu) * lax.rsqrt(var + LN_EPS)).reshape(B, T, RET_HEADS * RET_V_DIM).astype(o.dtype) * gain
    return (jax.nn.silu(g) * on) @ w_out


def retention_prompt(x, w_in, gain, w_out, pos):
    B, T, _ = x.shape
    q, k, v, g = retention_project(x, w_in, pos)
    n = T // RET_CHUNK

    def to_chunks(a):
        return a.reshape(B, n, RET_CHUNK, *a.shape[2:]).swapaxes(0, 1)

    state0 = jnp.zeros((B, RET_HEADS, RET_QK_DIM, RET_V_DIM), x.dtype)
    state, o = lax.scan(lambda s, c: retention_chunk(s, *c), state0,
                        (to_chunks(q), to_chunks(k), to_chunks(v)))
    o = o.swapaxes(0, 1).reshape(B, T, RET_HEADS, RET_V_DIM)
    return retention_output(o, g, gain, w_out), state


def retention_sample(x, w_in, gain, w_out, state, pos):
    q, k, v, g = retention_project(x, w_in, pos)
    new_state, o = retention_chunk(state, q, k, v)
    return retention_output(o, g, gain, w_out), new_state


def moe_tokens(xt, w_router, b_router, w_gate, w_up, w_down):
    n = xt.shape[0]
    probs = jax.nn.softmax((xt @ w_router).astype(F32), axis=-1)
    sel = probs + b_router.astype(F32)
    gscore = lax.top_k(sel.reshape(n, N_GROUPS, EXPERTS_PER_GROUP), 2)[0].sum(-1)
    gbest = jnp.argmax(gscore, -1)
    in_group = (jnp.arange(N_EXPERTS) // EXPERTS_PER_GROUP)[None, :] == gbest[:, None]
    _, eidx = lax.top_k(jnp.where(in_group, sel, -jnp.inf), TOPK_EXPERTS)
    gw = jnp.take_along_axis(probs, eidx, -1)
    gw = gw / jnp.sum(gw, -1, keepdims=True)
    gate = jnp.sum(jax.nn.one_hot(eidx, N_EXPERTS, dtype=F32) * gw[..., None], axis=1)
    h = jax.nn.silu(jnp.einsum('nd,edf->nef', xt, w_gate)) * jnp.einsum('nd,edf->nef', xt, w_up)
    return jnp.einsum('nef,efd->nd', h * gate[:, :, None].astype(h.dtype), w_down)


def moe(x, w_router, b_router, w_gate, w_up, w_down):
    return lax.map(lambda xb: moe_tokens(xb, w_router, b_router, w_gate, w_up, w_down), x)


def setup_inputs(seed: int = 0) -> dict:
    key = jax.random.key(seed)
    ks = jax.random.split(key, 24)
    n_pages = PAST_LEN // PAGE_SIZE
    used = DEC_BATCH * n_pages
    pool = used + used // 4

    def nrm(k, shape, scale):
        return jax.random.normal(k, shape, F32) * scale

    return {
        'x_prompt': nrm(ks[0], (BATCH, SEQ, D_MODEL), 1.0),
        'x_sample': nrm(ks[1], (DEC_BATCH, DEC_SEQ, D_MODEL), 1.0),
        'cache_k': nrm(ks[2], (N_ATTN_LAYERS, pool, PAGE_SIZE, ATTN_HEADS, ATTN_HEAD_DIM), 1.0),
        'cache_v': nrm(ks[3], (N_ATTN_LAYERS, pool, PAGE_SIZE, ATTN_HEADS, ATTN_HEAD_DIM), 1.0),
        'cache_idx_k': nrm(ks[4], (N_ATTN_LAYERS, pool, PAGE_SIZE, IDX_DIM), 1.0),
        'state_ret': nrm(ks[5], (N_RET_LAYERS, DEC_BATCH, RET_HEADS, RET_QK_DIM, RET_V_DIM), 0.3),
        'page_table': jax.random.permutation(ks[6], pool)[:used].reshape(DEC_BATCH, n_pages).astype(jnp.int32),
        'w_in_attn': nrm(ks[7], (N_ATTN_LAYERS, D_MODEL, ATTN_IN), D_MODEL ** -0.5),
        'w_out_attn': nrm(ks[8], (N_ATTN_LAYERS, ATTN_WIDTH, D_MODEL), ATTN_WIDTH ** -0.5 * DEEPNORM_BETA),
        'w_in_ret': nrm(ks[9], (N_RET_LAYERS, D_MODEL, RET_IN), D_MODEL ** -0.5),
        'ret_norm_gain': 1.0 + nrm(ks[10], (N_RET_LAYERS, RET_HEADS * RET_V_DIM), 0.02),
        'w_out_ret': nrm(ks[11], (N_RET_LAYERS, RET_HEADS * RET_V_DIM, D_MODEL), (RET_HEADS * RET_V_DIM) ** -0.5 * DEEPNORM_BETA),
        'w_router': nrm(ks[12], (D_MODEL, N_EXPERTS), D_MODEL ** -0.5),
        'b_router': nrm(ks[13], (N_EXPERTS,), 0.01),
        'w_exp_gate': nrm(ks[14], (DEPTH, N_EXPERTS, D_MODEL, EXPERT_DIM), D_MODEL ** -0.5),
        'w_exp_up': nrm(ks[15], (DEPTH, N_EXPERTS, D_MODEL, EXPERT_DIM), D_MODEL ** -0.5),
        'w_exp_down': nrm(ks[16], (DEPTH, N_EXPERTS, EXPERT_DIM, D_MODEL), EXPERT_DIM ** -0.5 * DEEPNORM_BETA),
        'ln_gain': 1.0 + nrm(ks[17], (DEPTH, 2, D_MODEL), 0.02),
        'ln_bias': nrm(ks[18], (DEPTH, 2, D_MODEL), 0.02),
    }


def reference(x_prompt, x_sample, cache_k, cache_v, cache_idx_k, state_ret, page_table,
              w_in_attn, w_out_attn, w_in_ret, ret_norm_gain, w_out_ret,
              w_router, b_router, w_exp_gate, w_exp_up, w_exp_down, ln_gain, ln_bias):
    T = x_prompt.shape[1]
    Ts = x_sample.shape[1]
    past = page_table.shape[1] * PAGE_SIZE
    pos_p = jnp.arange(T, dtype=jnp.int32)
    pos_s = past + jnp.arange(Ts, dtype=jnp.int32)
    topk_p = min(IDX_TOPK_MAX, T // 4)
    topk_s = min(IDX_TOPK_MAX, (past + Ts) // 4)

    xp, xs = x_prompt, x_sample
    kp_l, vp_l, ikp_l, ks_l, vs_l, iks_l, sp_l, ss_l = [], [], [], [], [], [], [], []
    for i in range(DEPTH):
        if i % N_MIXERS == 0:
            a = i // N_MIXERS
            mp, kp, vp, ikp = dsa_prompt(xp, w_in_attn[a], w_out_attn[a], pos_p, topk_p)
            ms, k_s, v_s, ik_s = dsa_sample(xs, w_in_attn[a], w_out_attn[a], cache_k[a], cache_v[a],
                                            cache_idx_k[a], page_table, pos_s, topk_s)
            kp_l.append(kp); vp_l.append(vp); ikp_l.append(ikp)
            ks_l.append(k_s); vs_l.append(v_s); iks_l.append(ik_s)
        else:
            r = i // N_MIXERS
            mp, sp = retention_prompt(xp, w_in_ret[r], ret_norm_gain[r], w_out_ret[r], pos_p)
            ms, ss = retention_sample(xs, w_in_ret[r], ret_norm_gain[r], w_out_ret[r], state_ret[r], pos_s)
            sp_l.append(sp); ss_l.append(ss)
        xp = layer_norm(DEEPNORM_ALPHA * xp + mp, ln_gain[i, 0], ln_bias[i, 0])
        xs = layer_norm(DEEPNORM_ALPHA * xs + ms, ln_gain[i, 0], ln_bias[i, 0])
        xp = layer_norm(DEEPNORM_ALPHA * xp + moe(xp, w_router, b_router, w_exp_gate[i], w_exp_up[i], w_exp_down[i]),
                        ln_gain[i, 1], ln_bias[i, 1])
        xs = layer_norm(DEEPNORM_ALPHA * xs + moe(xs, w_router, b_router, w_exp_gate[i], w_exp_up[i], w_exp_down[i]),
                        ln_gain[i, 1], ln_bias[i, 1])

    return (xp, xs, jnp.stack(kp_l), jnp.stack(vp_l), jnp.stack(ikp_l),
            jnp.stack(ks_l), jnp.stack(vs_l), jnp.stack(iks_l),
            jnp.stack(sp_l), jnp.stack(ss_l))
```

```python
import functools

import jax
import jax.numpy as jnp
from jax import lax
from jax.experimental import pallas as pl
from jax.experimental.pallas import tpu as pltpu

F32 = jnp.float32
BF16 = jnp.bfloat16
MXU_DTYPE = jnp.bfloat16

D_MODEL = 1024
PAGE_SIZE = 128
ATTN_HEADS = 16
ATTN_HEAD_DIM = 64
ATTN_WIDTH = ATTN_HEADS * ATTN_HEAD_DIM
IDX_HEADS = 8
IDX_DIM = 64
IDX_TOPK_MAX = 256
Q_BLOCK = 128
ROPE_THETA = 10000.0
RET_HEADS = 4
RET_QK_DIM = D_MODEL // RET_HEADS
RET_V_DIM = 2 * D_MODEL // RET_HEADS
RET_CHUNK = 128
N_EXPERTS = 16
N_GROUPS = 4
EXPERTS_PER_GROUP = N_EXPERTS // N_GROUPS
EXPERT_DIM = D_MODEL // 2
DEPTH = 2
DEEPNORM_ALPHA = (2 * DEPTH) ** 0.25
LN_EPS = 1e-5

LANES = 128
SUBLANES = 8
VMEM_LIMIT_BYTES = 58 * 1024 * 1024

NEG_INF = float("-inf")
NEG_BIG = -1e30


def _cparams(*sem):
    return pltpu.CompilerParams(dimension_semantics=sem, vmem_limit_bytes=VMEM_LIMIT_BYTES)


def _mm(a, b):
    return jnp.dot(a, b, preferred_element_type=F32)


def _mm_nt(a, b):
    return lax.dot_general(a, b, (((1,), (1,)), ((), ())), preferred_element_type=F32)


def _mm_tn(a, b):
    return lax.dot_general(a, b, (((0,), (0,)), ((), ())), preferred_element_type=F32)


def _layer_norm(z, g, b):
    mu = jnp.mean(z, axis=-1, keepdims=True)
    d = z - mu
    var = jnp.mean(d * d, axis=-1, keepdims=True)
    return d * lax.rsqrt(var + LN_EPS) * g + b


def _sigmoid(x):
    return 1.0 / (1.0 + jnp.exp(-x))


def _rope_tables(pos, head_dim, width):
    half = head_dim // 2
    inv = ROPE_THETA ** (-jnp.arange(0, head_dim, 2, dtype=F32) / head_dim)
    ang = pos.astype(F32)[:, None] * inv[None, :]
    cos, sin = jnp.cos(ang), jnp.sin(ang)
    cos_h = jnp.concatenate([cos, cos], axis=1)
    sin_h = jnp.concatenate([-sin, sin], axis=1)
    reps = width // head_dim
    return jnp.tile(cos_h, (1, reps)), jnp.tile(sin_h, (1, reps))


def _rope(y, cos, sin, half):
    w = y.shape[-1]
    lane = lax.broadcasted_iota(jnp.int32, y.shape, 1)
    first = (lane % (2 * half)) < half
    swapped = jnp.where(first, pltpu.roll(y, w - half, 1), pltpu.roll(y, half, 1))
    return y * cos + swapped * sin


def _proj_kernel(x_ref, cos_ref, sin_ref, *refs, groups):
    n_g = len(groups)
    w_refs, out_refs = refs[:n_g], refs[n_g:]
    xb = x_ref[...].astype(MXU_DTYPE)
    oi = 0
    for g, (rope_half, rope_w, outs) in enumerate(groups):
        y = _mm(xb, w_refs[g][...])
        if rope_half:
            yr = _rope(y[:, :rope_w], cos_ref[:, :rope_w], sin_ref[:, :rope_w], rope_half)
            y = yr if rope_w == y.shape[1] else jnp.concatenate([yr, y[:, rope_w:]], axis=1)
        for (c0, w, scale, dt) in outs:
            piece = y[:, c0:c0 + w]
            if scale != 1.0:
                piece = piece * scale
            out_refs[oi][...] = piece.astype(dt)
            oi += 1


def _token_tile(n, cap):
    tm = min(cap, n)
    assert n % tm == 0 and tm % SUBLANES == 0
    return tm


def _project(x2d, weights, groups, cos, sin, n_tab_blocks, tm):
    n = x2d.shape[0]
    out_shapes, out_specs = [], []
    for (_, _, outs) in groups:
        for (_, w, _, dt) in outs:
            out_shapes.append(jax.ShapeDtypeStruct((n, w), dt))
            out_specs.append(pl.BlockSpec((tm, w), lambda m: (m, 0)))
    tw = cos.shape[1]
    in_specs = [pl.BlockSpec((tm, D_MODEL), lambda m: (m, 0)),
                pl.BlockSpec((tm, tw), lambda m: (m % n_tab_blocks, 0)),
                pl.BlockSpec((tm, tw), lambda m: (m % n_tab_blocks, 0))]
    for w in weights:
        in_specs.append(pl.BlockSpec(w.shape, lambda m: (0, 0)))
    return pl.pallas_call(
        functools.partial(_proj_kernel, groups=groups),
        grid=(n // tm,),
        in_specs=in_specs,
        out_specs=out_specs,
        out_shape=out_shapes,
        compiler_params=_cparams("parallel"),
        name="proj",
    )(x2d, cos, sin, *weights)


def _row_count(pred):
    return jnp.sum(jnp.where(pred, 1.0, 0.0), axis=1, keepdims=True)


def _topk_select(score, k):
    r, s = score.shape
    score = jnp.where(score == 0.0, 0.0, score)
    bits = lax.bitcast_convert_type(score, jnp.int32)
    key = bits ^ ((bits >> 31) & jnp.int32(0x7FFFFFFF))
    kf = jnp.float32(k)

    def value_bit(i, t):
        cand = t + lax.shift_left(jnp.int32(1), jnp.int32(31) - i)
        return jnp.where(_row_count(key >= cand) >= kf, cand, t)

    t = lax.fori_loop(0, 32, value_bit, jnp.full((r, 1), jnp.iinfo(jnp.int32).min, jnp.int32))
    gt = key > t
    eq = key == t
    need = kf - _row_count(gt)
    col = lax.broadcasted_iota(jnp.int32, (r, s), 1)
    n_bits = max(1, (s - 1).bit_length())

    def index_bit(i, j):
        cand = j + lax.shift_left(jnp.int32(1), jnp.int32(n_bits - 1) - i)
        return jnp.where(_row_count(eq & (col < cand)) < need, cand, j)

    j = lax.fori_loop(0, n_bits, index_bit, jnp.zeros((r, 1), jnp.int32))
    keep = gt | (eq & (col <= j))
    return keep & (score > NEG_INF)


def _idx_mask_kernel(qi_ref, wi_ref, kk_ref, bias_ref, *, topk):
    i = pl.program_id(1)
    qi = qi_ref[...]
    kk = kk_ref[...]
    wi = wi_ref[...]
    qb, t_len = qi.shape[0], kk.shape[0]
    lane = lax.broadcasted_iota(jnp.int32, (qb, LANES), 1)
    zero = jnp.zeros((qb, LANES), qi.dtype)
    score = jnp.zeros((qb, t_len), F32)
    for p in range(IDX_HEADS // 2):
        qp = qi[:, LANES * p:LANES * (p + 1)]
        for half in range(2):
            lhs = jnp.where(lane < IDX_DIM, qp, zero) if half == 0 else jnp.where(lane >= IDX_DIM, qp, zero)
            h = 2 * p + half
            score = score + jnp.maximum(_mm_nt(lhs, kk), 0.0) * wi[:, h:h + 1]
    row = i * qb + lax.broadcasted_iota(jnp.int32, (qb, t_len), 0)
    col = lax.broadcasted_iota(jnp.int32, (qb, t_len), 1)
    score = jnp.where(col <= row, score, NEG_INF)
    keep = _topk_select(score, topk)
    bias_ref[...] = jnp.where(keep, 0.0, NEG_INF).astype(bias_ref.dtype)


def _prompt_index_mask(qi, wi, kk, batch, t_len, topk):
    nb = t_len // Q_BLOCK
    return pl.pallas_call(
        functools.partial(_idx_mask_kernel, topk=topk),
        grid=(batch, nb),
        in_specs=[pl.BlockSpec((Q_BLOCK, qi.shape[1]), lambda b, i: (b * nb + i, 0)),
                  pl.BlockSpec((Q_BLOCK, LANES), lambda b, i: (b * nb + i, 0)),
                  pl.BlockSpec((t_len, LANES), lambda b, i: (b, 0))],
        out_specs=pl.BlockSpec((Q_BLOCK, t_len), lambda b, i: (b * nb + i, 0)),
        out_shape=jax.ShapeDtypeStruct((batch * t_len, t_len), BF16),
        compiler_params=_cparams("parallel", "arbitrary"),
        name="prompt_index_mask",
    )(qi, wi, kk)


def _attn_kernel(q_ref, bias_ref, k_ref, v_ref, o_ref):
    qb = q_ref.shape[0]
    bias = bias_ref[...].astype(F32)
    bias2 = jnp.concatenate([bias, bias], axis=0)
    lane = lax.broadcasted_iota(jnp.int32, (qb, LANES), 1)
    low = lane < ATTN_HEAD_DIM
    zero = jnp.zeros((qb, LANES), q_ref.dtype)
    for p in range(ATTN_HEADS // 2):
        sl = slice(LANES * p, LANES * (p + 1))
        qp = q_ref[:, sl]
        lhs = jnp.concatenate([jnp.where(low, qp, zero), jnp.where(low, zero, qp)], axis=0)
        s = _mm_nt(lhs, k_ref[:, sl]) + bias2
        m = jnp.max(s, axis=1, keepdims=True)
        e = jnp.exp(s - m)
        l = jnp.sum(e, axis=1, keepdims=True)
        pv = _mm(e.astype(MXU_DTYPE), v_ref[:, sl]) / l
        o_ref[:, sl] = jnp.where(low, pv[:qb], pv[qb:]).astype(o_ref.dtype)


def _prompt_attention(q, bias, k, v, batch, t_len):
    nb = t_len // Q_BLOCK
    return pl.pallas_call(
        _attn_kernel,
        grid=(batch, nb),
        in_specs=[pl.BlockSpec((Q_BLOCK, ATTN_WIDTH), lambda b, i: (b * nb + i, 0)),
                  pl.BlockSpec((Q_BLOCK, t_len), lambda b, i: (b * nb + i, 0)),
                  pl.BlockSpec((t_len, ATTN_WIDTH), lambda b, i: (b, 0)),
                  pl.BlockSpec((t_len, ATTN_WIDTH), lambda b, i: (b, 0))],
        out_specs=pl.BlockSpec((Q_BLOCK, ATTN_WIDTH), lambda b, i: (b * nb + i, 0)),
        out_shape=jax.ShapeDtypeStruct((batch * t_len, ATTN_WIDTH), MXU_DTYPE),
        compiler_params=_cparams("parallel", "arbitrary"),
        name="prompt_attention",
    )(q, bias, k, v)


def _sample_score_kernel(pt_ref, qi_ref, wi_ref, kin_ref, *refs, n_pages):
    del pt_ref
    page_refs, out_ref = refs[:n_pages], refs[n_pages]
    ts = qi_ref.shape[0]
    qi = qi_ref[...]
    wi = wi_ref[...]
    lhs = jnp.concatenate([qi[:, IDX_DIM * h:IDX_DIM * (h + 1)] for h in range(IDX_HEADS)],
                          axis=0).astype(MXU_DTYPE)
    wcol = jnp.concatenate([wi[:, h:h + 1] for h in range(IDX_HEADS)], axis=0)

    def head_sum(s):
        w = jnp.maximum(s, 0.0) * wcol
        acc = w[0:ts]
        for h in range(1, IDX_HEADS):
            acc = acc + w[ts * h:ts * (h + 1)]
        return acc

    for j in range(n_pages):
        kp = page_refs[j][0].astype(MXU_DTYPE)
        out_ref[0, :, PAGE_SIZE * j:PAGE_SIZE * (j + 1)] = head_sum(_mm_nt(lhs, kp))
    kn = kin_ref[...].astype(MXU_DTYPE)
    kn = jnp.concatenate([kn, jnp.zeros((LANES - ts, IDX_DIM), MXU_DTYPE)], axis=0)
    sn = head_sum(_mm_nt(lhs, kn))
    row = lax.broadcasted_iota(jnp.int32, sn.shape, 0)
    col = lax.broadcasted_iota(jnp.int32, sn.shape, 1)
    past = PAGE_SIZE * n_pages
    out_ref[0, :, past:past + LANES] = jnp.where(col <= row, sn, NEG_INF)


def _sample_scores(page_table, qi, wi, ki_new, cache_ik, dec_batch, ts):
    n_pages = page_table.shape[1]
    past = n_pages * PAGE_SIZE

    def page_spec(j):
        return pl.BlockSpec((1, PAGE_SIZE, IDX_DIM), lambda b, pt: (pt[b, j], 0, 0))

    grid_spec = pltpu.PrefetchScalarGridSpec(
        num_scalar_prefetch=1,
        grid=(dec_batch,),
        in_specs=[pl.BlockSpec((ts, qi.shape[1]), lambda b, pt: (b, 0)),
                  pl.BlockSpec((ts, LANES), lambda b, pt: (b, 0)),
                  pl.BlockSpec((ts, IDX_DIM), lambda b, pt: (b, 0))]
                 + [page_spec(j) for j in range(n_pages)],
        out_specs=pl.BlockSpec((1, ts, past + LANES), lambda b, pt: (b, 0, 0)),
    )
    return pl.pallas_call(
        functools.partial(_sample_score_kernel, n_pages=n_pages),
        grid_spec=grid_spec,
        out_shape=jax.ShapeDtypeStruct((dec_batch, ts, past + LANES), F32),
        compiler_params=_cparams("arbitrary"),
        name="sample_scores",
    )(page_table, qi, wi, ki_new, *([cache_ik] * n_pages))


def _mask_kernel(score_ref, bias_ref, *, topk):
    keep = _topk_select(score_ref[...], topk)
    bias_ref[...] = jnp.where(keep, 0.0, NEG_INF).astype(bias_ref.dtype)


def _sample_mask(scores2d, topk):
    n, s = scores2d.shape
    rb = _token_tile(n, Q_BLOCK)
    return pl.pallas_call(
        functools.partial(_mask_kernel, topk=topk),
        grid=(n // rb,),
        in_specs=[pl.BlockSpec((rb, s), lambda i: (i, 0))],
        out_specs=pl.BlockSpec((rb, s), lambda i: (i, 0)),
        out_shape=jax.ShapeDtypeStruct((n, s), BF16),
        compiler_params=_cparams("parallel"),
        name="sample_mask",
    )(scores2d)


def _sample_attn_kernel(pt_ref, q_ref, bias_ref, biasn_ref, kn_ref, vn_ref, *refs, pp):
    del pt_ref
    k_pages, v_pages = refs[:pp], refs[pp:2 * pp]
    o_ref, m_sc, l_sc, acc_sc = refs[2 * pp:]
    j = pl.program_id(1)
    ts = q_ref.shape[0]
    rows = ATTN_HEADS * ts
    q = q_ref[...]
    dh = ATTN_HEAD_DIM

    @pl.when(j == 0)
    def _():
        m_sc[...] = jnp.full(m_sc.shape, NEG_BIG, F32)
        l_sc[...] = jnp.zeros(l_sc.shape, F32)
        acc_sc[...] = jnp.zeros(acc_sc.shape, F32)

    def online_update(s, pv_fn):
        m_old = m_sc[...]
        m_new = jnp.maximum(m_old, jnp.max(s, axis=1, keepdims=True))
        alpha = jnp.exp(m_old - m_new)
        p = jnp.exp(s - m_new)
        l_sc[...] = alpha * l_sc[...] + jnp.sum(p, axis=1, keepdims=True)
        acc_sc[...] = alpha * acc_sc[...] + pv_fn(p.astype(MXU_DTYPE))
        m_sc[...] = m_new

    q_heads = [q[:, dh * h:dh * (h + 1)].astype(MXU_DTYPE) for h in range(ATTN_HEADS)]
    for pg in range(pp):
        k_ref, v_ref = k_pages[pg], v_pages[pg]

        def head_rows(ref, h):
            return ref[0, pl.ds(h, PAGE_SIZE, stride=ATTN_HEADS), :].astype(MXU_DTYPE)

        s = jnp.concatenate([_mm_nt(q_heads[h], head_rows(k_ref, h)) for h in range(ATTN_HEADS)], axis=0)
        bias = bias_ref[0, :, PAGE_SIZE * pg:PAGE_SIZE * (pg + 1)].astype(F32)
        s = s + jnp.concatenate([bias] * ATTN_HEADS, axis=0)

        def pv_pages(p, v_ref=v_ref):
            return jnp.concatenate([_mm(p[ts * h:ts * (h + 1)], head_rows(v_ref, h))
                                    for h in range(ATTN_HEADS)], axis=0)
        online_update(s, pv_pages)

    @pl.when(j == pl.num_programs(1) - 1)
    def _():
        pad = jnp.zeros((LANES - ts, ATTN_WIDTH), F32)
        kn = jnp.concatenate([kn_ref[...], pad], axis=0).astype(MXU_DTYPE)
        vn = jnp.concatenate([vn_ref[...], pad], axis=0).astype(MXU_DTYPE)
        lane = lax.broadcasted_iota(jnp.int32, (ts, ATTN_WIDTH), 1)
        qbd = jnp.concatenate([jnp.where((lane >= dh * h) & (lane < dh * (h + 1)), q, 0.0)
                               for h in range(ATTN_HEADS)], axis=0).astype(MXU_DTYPE)
        s = _mm_nt(qbd, kn) + jnp.concatenate([biasn_ref[0].astype(F32)] * ATTN_HEADS, axis=0)

        def pv_new(p):
            full = _mm(p, vn)
            return jnp.concatenate([full[ts * h:ts * (h + 1), dh * h:dh * (h + 1)]
                                    for h in range(ATTN_HEADS)], axis=0)
        online_update(s, pv_new)
        out = acc_sc[...] / l_sc[...]
        o_ref[...] = jnp.concatenate([out[ts * h:ts * (h + 1)] for h in range(ATTN_HEADS)],
                                     axis=1).astype(o_ref.dtype)


def _sample_attention(page_table, q, bias, k_new, v_new, cache_k2, cache_v2, dec_batch, ts, pp):
    n_pages = page_table.shape[1]
    assert n_pages % pp == 0
    past = n_pages * PAGE_SIZE
    rows_per_page = PAGE_SIZE * ATTN_HEADS

    def page_spec(g):
        return pl.BlockSpec((1, rows_per_page, ATTN_HEAD_DIM), lambda b, j, pt: (pt[b, j * pp + g], 0, 0))

    grid_spec = pltpu.PrefetchScalarGridSpec(
        num_scalar_prefetch=1,
        grid=(dec_batch, n_pages // pp),
        in_specs=[pl.BlockSpec((ts, ATTN_WIDTH), lambda b, j, pt: (b, 0)),
                  pl.BlockSpec((1, ts, pp * PAGE_SIZE), lambda b, j, pt: (b, 0, j)),
                  pl.BlockSpec((1, ts, LANES), lambda b, j, pt: (b, 0, past // LANES)),
                  pl.BlockSpec((ts, ATTN_WIDTH), lambda b, j, pt: (b, 0)),
                  pl.BlockSpec((ts, ATTN_WIDTH), lambda b, j, pt: (b, 0))]
                 + [page_spec(g) for g in range(pp)] * 2,
        out_specs=pl.BlockSpec((ts, ATTN_WIDTH), lambda b, j, pt: (b, 0)),
        scratch_shapes=[pltpu.VMEM((ATTN_HEADS * ts, 1), F32),
                        pltpu.VMEM((ATTN_HEADS * ts, 1), F32),
                        pltpu.VMEM((ATTN_HEADS * ts, ATTN_HEAD_DIM), F32)],
    )
    return pl.pallas_call(
        functools.partial(_sample_attn_kernel, pp=pp),
        grid_spec=grid_spec,
        out_shape=jax.ShapeDtypeStruct((dec_batch * ts, ATTN_WIDTH), MXU_DTYPE),
        compiler_params=_cparams("parallel", "arbitrary"),
        name="sample_attention",
    )(page_table, q, bias, bias, k_new, v_new, *([cache_k2] * pp), *([cache_v2] * pp))


def _outproj_ln_kernel(o_ref, x_ref, w_ref, g_ref, b_ref, y_ref):
    m = _mm(o_ref[...].astype(MXU_DTYPE), w_ref[...])
    y_ref[...] = _layer_norm(DEEPNORM_ALPHA * x_ref[...] + m, g_ref[...], b_ref[...])


def _outproj_ln(o, x2d, w, g, b, tm):
    n = x2d.shape[0]
    return pl.pallas_call(
        _outproj_ln_kernel,
        grid=(n // tm,),
        in_specs=[pl.BlockSpec((tm, o.shape[1]), lambda m: (m, 0)),
                  pl.BlockSpec((tm, D_MODEL), lambda m: (m, 0)),
                  pl.BlockSpec(w.shape, lambda m: (0, 0)),
                  pl.BlockSpec((1, D_MODEL), lambda m: (0, 0)),
                  pl.BlockSpec((1, D_MODEL), lambda m: (0, 0))],
        out_specs=pl.BlockSpec((tm, D_MODEL), lambda m: (m, 0)),
        out_shape=jax.ShapeDtypeStruct((n, D_MODEL), F32),
        compiler_params=_cparams("parallel"),
        name="outproj_ln",
    )(o, x2d, w, g, b)


def _retention_tables(chunk):
    lg = jnp.log(1.0 - 2.0 ** (-5.0 - jnp.arange(RET_HEADS, dtype=F32)))
    i = jnp.arange(chunk, dtype=F32)
    diff = i[:, None] - i[None, :]
    dmask = jnp.where(diff[None] >= 0, jnp.exp(jnp.maximum(diff, 0.0)[None] * lg[:, None, None]), 0.0)
    cross = jnp.exp((i + 1.0)[:, None] * lg[None, :])
    kdec = jnp.exp((chunk - 1.0 - i)[:, None] * lg[None, :])
    cdec = jnp.exp(chunk * lg)
    pad = jnp.zeros((chunk, LANES - RET_HEADS), F32)
    cross = jnp.concatenate([cross, pad], axis=1)
    kdec = jnp.concatenate([kdec, pad], axis=1)
    cdec = jnp.broadcast_to(cdec[:, None, None], (RET_HEADS, 1, RET_V_DIM))
    return dmask.astype(F32), cross, kdec, cdec


def _retention_kernel(q_ref, k_ref, v_ref, s0_ref, dmask_ref, cross_ref, kdec_ref, cdec_ref,
                      o_ref, s_out_ref, state_sc):
    c = pl.program_id(1)

    @pl.when(c == 0)
    def _():
        state_sc[...] = s0_ref[0]

    cross = cross_ref[...]
    kdec = kdec_ref[...]
    for h in range(RET_HEADS):
        qs = slice(RET_QK_DIM * h, RET_QK_DIM * (h + 1))
        vs = slice(RET_V_DIM * h, RET_V_DIM * (h + 1))
        qh = q_ref[:, qs]
        kh = k_ref[:, qs]
        qb = qh.astype(MXU_DTYPE)
        vb = v_ref[:, vs].astype(MXU_DTYPE)
        state = state_sc[h]
        inner = _mm_nt(qb, kh.astype(MXU_DTYPE)) * dmask_ref[h]
        o = _mm(inner.astype(MXU_DTYPE), vb) + _mm(qb, state.astype(MXU_DTYPE)) * cross[:, h:h + 1]
        o_ref[:, vs] = o
        kd = (kh * kdec[:, h:h + 1]).astype(MXU_DTYPE)
        state_sc[h] = cdec_ref[h] * state + _mm_tn(kd, vb)

    @pl.when(c == pl.num_programs(1) - 1)
    def _():
        s_out_ref[0] = state_sc[...]


def _retention_scan(q, k, v, state0, batch, n_chunks, chunk):
    dmask, cross, kdec, cdec = _retention_tables(chunk)
    hq, hv = RET_HEADS * RET_QK_DIM, RET_HEADS * RET_V_DIM
    state_block = (1, RET_HEADS, RET_QK_DIM, RET_V_DIM)
    return pl.pallas_call(
        _retention_kernel,
        grid=(batch, n_chunks),
        in_specs=[pl.BlockSpec((chunk, hq), lambda b, c: (b * n_chunks + c, 0)),
                  pl.BlockSpec((chunk, hq), lambda b, c: (b * n_chunks + c, 0)),
                  pl.BlockSpec((chunk, hv), lambda b, c: (b * n_chunks + c, 0)),
                  pl.BlockSpec(state_block, lambda b, c: (b, 0, 0, 0)),
                  pl.BlockSpec(dmask.shape, lambda b, c: (0, 0, 0)),
                  pl.BlockSpec(cross.shape, lambda b, c: (0, 0)),
                  pl.BlockSpec(kdec.shape, lambda b, c: (0, 0)),
                  pl.BlockSpec(cdec.shape, lambda b, c: (0, 0, 0))],
        out_specs=[pl.BlockSpec((chunk, hv), lambda b, c: (b * n_chunks + c, 0)),
                   pl.BlockSpec(state_block, lambda b, c: (b, 0, 0, 0))],
        out_shape=[jax.ShapeDtypeStruct((batch * n_chunks * chunk, hv), F32),
                   jax.ShapeDtypeStruct((batch,) + state_block[1:], F32)],
        scratch_shapes=[pltpu.VMEM(state_block[1:], F32)],
        compiler_params=_cparams("parallel", "arbitrary"),
        name="retention_scan",
    )(q, k, v, state0, dmask, cross, kdec, cdec)


def _ret_out_kernel(o_ref, gate_ref, gain_ref, x_ref, w_ref, g_ref, b_ref, y_ref):
    pieces = []
    for h in range(RET_HEADS):
        vs = slice(RET_V_DIM * h, RET_V_DIM * (h + 1))
        of = o_ref[:, vs]
        mu = jnp.mean(of, axis=-1, keepdims=True)
        d = of - mu
        var = jnp.mean(d * d, axis=-1, keepdims=True)
        on = d * lax.rsqrt(var + LN_EPS) * gain_ref[:, vs]
        gt = gate_ref[:, vs]
        pieces.append(((gt * _sigmoid(gt)) * on).astype(MXU_DTYPE))
    m = _mm(jnp.concatenate(pieces, axis=1), w_ref[...])
    y_ref[...] = _layer_norm(DEEPNORM_ALPHA * x_ref[...] + m, g_ref[...], b_ref[...])


def _retention_output_ln(o, gate, gain, x2d, w, g, b, tm):
    n = x2d.shape[0]
    hv = RET_HEADS * RET_V_DIM
    return pl.pallas_call(
        _ret_out_kernel,
        grid=(n // tm,),
        in_specs=[pl.BlockSpec((tm, hv), lambda m: (m, 0)),
                  pl.BlockSpec((tm, hv), lambda m: (m, 0)),
                  pl.BlockSpec((1, hv), lambda m: (0, 0)),
                  pl.BlockSpec((tm, D_MODEL), lambda m: (m, 0)),
                  pl.BlockSpec(w.shape, lambda m: (0, 0)),
                  pl.BlockSpec((1, D_MODEL), lambda m: (0, 0)),
                  pl.BlockSpec((1, D_MODEL), lambda m: (0, 0))],
        out_specs=pl.BlockSpec((tm, D_MODEL), lambda m: (m, 0)),
        out_shape=jax.ShapeDtypeStruct((n, D_MODEL), F32),
        compiler_params=_cparams("parallel"),
        name="retention_out_ln",
    )(o, gate, gain, x2d, w, g, b)


def _route(logits, b_router):
    shape = logits.shape
    lane = lax.broadcasted_iota(jnp.int32, shape, 1)
    lane_f = lane.astype(F32)
    valid = lane < N_EXPERTS
    lg = jnp.where(valid, logits, NEG_INF)
    ex = jnp.exp(lg - jnp.max(lg, axis=1, keepdims=True))
    probs = ex / jnp.sum(ex, axis=1, keepdims=True)
    sel = probs + b_router
    far = jnp.float32(LANES)

    def first_max(v):
        m = jnp.max(v, axis=1, keepdims=True)
        idx = jnp.min(jnp.where(v == m, lane_f, far), axis=1, keepdims=True)
        return m, idx

    gscore = []
    for g in range(N_GROUPS):
        in_g = (lane >= EXPERTS_PER_GROUP * g) & (lane < EXPERTS_PER_GROUP * (g + 1))
        sg = jnp.where(in_g, sel, NEG_INF)
        m1, i1 = first_max(sg)
        m2 = jnp.max(jnp.where(lane_f == i1, NEG_INF, sg), axis=1, keepdims=True)
        gscore.append(m1 + m2)
    best, gbest = gscore[0], jnp.zeros_like(gscore[0])
    for g in range(1, N_GROUPS):
        better = gscore[g] > best
        best = jnp.where(better, gscore[g], best)
        gbest = jnp.where(better, jnp.float32(g), gbest)
    lo = gbest * EXPERTS_PER_GROUP
    in_best = (lane_f >= lo) & (lane_f < lo + EXPERTS_PER_GROUP)
    sm = jnp.where(in_best, sel, NEG_INF)
    _, e1 = first_max(sm)
    _, e2 = first_max(jnp.where(lane_f == e1, NEG_INF, sm))
    is1, is2 = lane_f == e1, lane_f == e2
    p1 = jnp.sum(jnp.where(is1, probs, 0.0), axis=1, keepdims=True)
    p2 = jnp.sum(jnp.where(is2, probs, 0.0), axis=1, keepdims=True)
    den = p1 + p2
    return jnp.where(is1, p1 / den, 0.0) + jnp.where(is2, p2 / den, 0.0)


def _moe_kernel(x_ref, wr_ref, br_ref, wg_ref, wu_ref, wd_ref, g_ref, b_ref, y_ref,
                xb_sc, gate_sc, acc_sc):
    e = pl.program_id(1)

    @pl.when(e == 0)
    def _():
        x = x_ref[...]
        xb_sc[...] = x.astype(MXU_DTYPE)
        logits = jnp.dot(x, wr_ref[...], preferred_element_type=F32, precision=lax.Precision.HIGHEST)
        gate_sc[...] = _route(logits, br_ref[...])
        acc_sc[...] = jnp.zeros(acc_sc.shape, F32)

    xb = xb_sc[...]
    hg = _mm(xb, wg_ref[0])
    hu = _mm(xb, wu_ref[0])
    gate = gate_sc[...]
    lane = lax.broadcasted_iota(jnp.int32, gate.shape, 1)
    ge = jnp.sum(jnp.where(lane == e, gate, 0.0), axis=1, keepdims=True)
    h = (hg * _sigmoid(hg)) * hu * ge
    acc_sc[...] += _mm(h.astype(MXU_DTYPE), wd_ref[0])

    @pl.when(e == pl.num_programs(1) - 1)
    def _():
        y_ref[...] = _layer_norm(DEEPNORM_ALPHA * x_ref[...] + acc_sc[...], g_ref[...], b_ref[...])


def _moe_ln(x2d, wr, br, wg, wu, wd, g, b, tm):
    n = x2d.shape[0]
    return pl.pallas_call(
        _moe_kernel,
        grid=(n // tm, N_EXPERTS),
        in_specs=[pl.BlockSpec((tm, D_MODEL), lambda m, e: (m, 0)),
                  pl.BlockSpec(wr.shape, lambda m, e: (0, 0)),
                  pl.BlockSpec(br.shape, lambda m, e: (0, 0)),
                  pl.BlockSpec((1, D_MODEL, EXPERT_DIM), lambda m, e: (e, 0, 0)),
                  pl.BlockSpec((1, D_MODEL, EXPERT_DIM), lambda m, e: (e, 0, 0)),
                  pl.BlockSpec((1, EXPERT_DIM, D_MODEL), lambda m, e: (e, 0, 0)),
                  pl.BlockSpec((1, D_MODEL), lambda m, e: (0, 0)),
                  pl.BlockSpec((1, D_MODEL), lambda m, e: (0, 0))],
        out_specs=pl.BlockSpec((tm, D_MODEL), lambda m, e: (m, 0)),
        out_shape=jax.ShapeDtypeStruct((n, D_MODEL), F32),
        scratch_shapes=[pltpu.VMEM((tm, D_MODEL), MXU_DTYPE),
                        pltpu.VMEM((tm, LANES), F32),
                        pltpu.VMEM((tm, D_MODEL), F32)],
        compiler_params=_cparams("parallel", "arbitrary"),
        name="moe_ln",
    )(x2d, wr, br, wg, wu, wd, g, b)


def _attn_weights(w_in):
    w = ATTN_WIDTH
    qi_w = IDX_HEADS * IDX_DIM
    wq, wk, wv = w_in[:, :w], w_in[:, w:2 * w], w_in[:, 2 * w:3 * w]
    wqi = w_in[:, 3 * w:3 * w + qi_w]
    wki = w_in[:, 3 * w + qi_w:3 * w + qi_w + IDX_DIM]
    wwi = w_in[:, 3 * w + qi_w + IDX_DIM:]
    pad = jnp.zeros((D_MODEL, LANES - IDX_HEADS), w_in.dtype)
    widx = jnp.concatenate([wqi, wki, wki, wwi, pad], axis=1)
    return [a.astype(MXU_DTYPE) for a in (wq, wk, wv, widx)]


def _attn_groups(q_dtype):
    half = ATTN_HEAD_DIM // 2
    qi_w = IDX_HEADS * IDX_DIM
    qk_scale = ATTN_HEAD_DIM ** -0.5
    idx_scale = IDX_DIM ** -0.5
    return (
        (half, ATTN_WIDTH, ((0, ATTN_WIDTH, qk_scale, q_dtype),)),
        (half, ATTN_WIDTH, ((0, ATTN_WIDTH, 1.0, F32), (0, ATTN_WIDTH, 1.0, MXU_DTYPE))),
        (0, 0, ((0, ATTN_WIDTH, 1.0, F32), (0, ATTN_WIDTH, 1.0, MXU_DTYPE))),
        (half, qi_w + LANES, ((0, qi_w, idx_scale, q_dtype),
                              (qi_w, IDX_DIM, 1.0, F32),
                              (qi_w, LANES, 1.0, MXU_DTYPE),
                              (qi_w + LANES, LANES, IDX_HEADS ** -0.5, F32))),
    )


def _ret_weights(w_in):
    hq, hv = RET_HEADS * RET_QK_DIM, RET_HEADS * RET_V_DIM
    parts = (w_in[:, :hq], w_in[:, hq:2 * hq], w_in[:, 2 * hq:2 * hq + hv], w_in[:, 2 * hq + hv:])
    return [a.astype(MXU_DTYPE) for a in parts]


def _ret_groups():
    half = RET_QK_DIM // 2
    hq, hv = RET_HEADS * RET_QK_DIM, RET_HEADS * RET_V_DIM
    return (
        (half, hq, ((0, hq, 1.0, F32),)),
        (half, hq, ((0, hq, RET_QK_DIM ** -0.5, F32),)),
        (0, 0, ((0, hv, 1.0, F32),)),
        (0, 0, ((0, hv, 1.0, F32),)),
    )


def kernel(x_prompt, x_sample, cache_k, cache_v, cache_idx_k, state_ret, page_table, w_in_attn, w_out_attn, w_in_ret, ret_norm_gain, w_out_ret, w_router, b_router, w_exp_gate, w_exp_up, w_exp_down, ln_gain, ln_bias):
    batch, t_len, _ = x_prompt.shape
    dec_batch, ts, _ = x_sample.shape
    n_pages = page_table.shape[1]
    past = n_pages * PAGE_SIZE
    pool = cache_k.shape[1]
    n_p, n_s = batch * t_len, dec_batch * ts
    topk_p = min(IDX_TOPK_MAX, t_len // 4)
    topk_s = min(IDX_TOPK_MAX, (past + ts) // 4)
    tm_p = _token_tile(t_len, 512)
    tm_s = _token_tile(n_s, 512)
    pos_p = jnp.arange(t_len, dtype=jnp.int32)
    pos_s = past + (jnp.arange(tm_s, dtype=jnp.int32) % ts)

    xp = x_prompt.reshape(n_p, D_MODEL)
    xs = x_sample.reshape(n_s, D_MODEL)

    wr = jnp.concatenate([w_router, jnp.zeros((D_MODEL, LANES - N_EXPERTS), F32)], axis=1)
    br = jnp.concatenate([b_router, jnp.zeros((LANES - N_EXPERTS,), F32)])[None, :]

    def moe(x2d, i, tm):
        return _moe_ln(x2d, wr, br, w_exp_gate[i].astype(MXU_DTYPE), w_exp_up[i].astype(MXU_DTYPE),
                       w_exp_down[i].astype(MXU_DTYPE), ln_gain[i, 1][None, :], ln_bias[i, 1][None, :], tm)

    a = 0
    aw = _attn_weights(w_in_attn[a])
    w_out = w_out_attn[a].astype(MXU_DTYPE)
    g0, b0 = ln_gain[0, 0][None, :], ln_bias[0, 0][None, :]

    cos_p, sin_p = _rope_tables(pos_p, ATTN_HEAD_DIM, ATTN_WIDTH)
    q_p, k_p, kb_p, v_p, vb_p, qi_p, ki_p, kk_p, wi_p = _project(
        xp, aw, _attn_groups(MXU_DTYPE), cos_p, sin_p, t_len // tm_p, tm_p)
    bias_p = _prompt_index_mask(qi_p, wi_p, kk_p, batch, t_len, topk_p)
    o_p = _prompt_attention(q_p, bias_p, kb_p, vb_p, batch, t_len)
    xp = _outproj_ln(o_p, xp, w_out, g0, b0, tm_p)

    cos_s, sin_s = _rope_tables(pos_s, ATTN_HEAD_DIM, ATTN_WIDTH)
    q_s, k_s, _, v_s, _, qi_s, ki_s, _, wi_s = _project(
        xs, aw, _attn_groups(F32), cos_s, sin_s, 1, tm_s)
    scores_s = _sample_scores(page_table, qi_s, wi_s, ki_s, cache_idx_k[a], dec_batch, ts)
    bias_s = _sample_mask(scores_s.reshape(n_s, past + LANES), topk_s).reshape(dec_batch, ts, past + LANES)
    cache_k2 = cache_k[a].reshape(pool, PAGE_SIZE * ATTN_HEADS, ATTN_HEAD_DIM)
    cache_v2 = cache_v[a].reshape(pool, PAGE_SIZE * ATTN_HEADS, ATTN_HEAD_DIM)
    pp = 4 if n_pages % 4 == 0 else 1
    o_s = _sample_attention(page_table, q_s, bias_s, k_s, v_s, cache_k2, cache_v2, dec_batch, ts, pp)
    xs = _outproj_ln(o_s, xs, w_out, g0, b0, tm_s)

    new_k_p = k_p.reshape(1, batch, t_len, ATTN_HEADS, ATTN_HEAD_DIM)
    new_v_p = v_p.reshape(1, batch, t_len, ATTN_HEADS, ATTN_HEAD_DIM)
    new_ik_p = ki_p.reshape(1, batch, t_len, IDX_DIM)
    new_k_s = k_s.reshape(1, dec_batch, ts, ATTN_HEADS, ATTN_HEAD_DIM)
    new_v_s = v_s.reshape(1, dec_batch, ts, ATTN_HEADS, ATTN_HEAD_DIM)
    new_ik_s = ki_s.reshape(1, dec_batch, ts, IDX_DIM)

    xp = moe(xp, 0, _token_tile(n_p, 1024))
    xs = moe(xs, 0, _token_tile(n_s, 1024))

    r = 0
    rw = _ret_weights(w_in_ret[r])
    w_out_r = w_out_ret[r].astype(MXU_DTYPE)
    gain = ret_norm_gain[r][None, :]
    g1, b1 = ln_gain[1, 0][None, :], ln_bias[1, 0][None, :]
    hq = RET_HEADS * RET_QK_DIM
    tm_rp = _token_tile(t_len, 256)
    tm_rs = _token_tile(n_s, 256)

    cos_p, sin_p = _rope_tables(pos_p, RET_QK_DIM, hq)
    rq, rk, rv, rg = _project(xp, rw, _ret_groups(), cos_p, sin_p, t_len // tm_rp, tm_rp)
    state0 = jnp.zeros((batch, RET_HEADS, RET_QK_DIM, RET_V_DIM), F32)
    ro, state_p = _retention_scan(rq, rk, rv, state0, batch, t_len // RET_CHUNK, RET_CHUNK)
    xp = _retention_output_ln(ro, rg, gain, xp, w_out_r, g1, b1, tm_rp)

    cos_s, sin_s = _rope_tables(past + (jnp.arange(tm_rs, dtype=jnp.int32) % ts), RET_QK_DIM, hq)
    rq, rk, rv, rg = _project(xs, rw, _ret_groups(), cos_s, sin_s, 1, tm_rs)
    ro, state_s = _retention_scan(rq, rk, rv, state_ret[r], dec_batch, 1, ts)
    xs = _retention_output_ln(ro, rg, gain, xs, w_out_r, g1, b1, tm_rs)

    xp = moe(xp, 1, _token_tile(n_p, 1024))
    xs = moe(xs, 1, _token_tile(n_s, 1024))

    return (xp.reshape(batch, t_len, D_MODEL), xs.reshape(dec_batch, ts, D_MODEL),
            new_k_p, new_v_p, new_ik_p, new_k_s, new_v_s, new_ik_s,
            state_p[None], state_s[None])
```

```python
import functools

import jax
import jax.numpy as jnp
from jax import lax
from jax.experimental import pallas as pl
from jax.experimental.pallas import tpu as pltpu

F32 = jnp.float32
BF16 = jnp.bfloat16
MXU_DTYPE = jnp.bfloat16

D_MODEL = 1024
PAGE_SIZE = 128
ATTN_HEADS = 16
ATTN_HEAD_DIM = 64
ATTN_WIDTH = ATTN_HEADS * ATTN_HEAD_DIM
IDX_HEADS = 8
IDX_DIM = 64
IDX_TOPK_MAX = 256
Q_BLOCK = 128
ROPE_THETA = 10000.0
RET_HEADS = 4
RET_QK_DIM = D_MODEL // RET_HEADS
RET_V_DIM = 2 * D_MODEL // RET_HEADS
RET_CHUNK = 128
N_EXPERTS = 16
N_GROUPS = 4
EXPERTS_PER_GROUP = N_EXPERTS // N_GROUPS
EXPERT_DIM = D_MODEL // 2
DEPTH = 2
DEEPNORM_ALPHA = (2 * DEPTH) ** 0.25
LN_EPS = 1e-5

LANES = 128
SUBLANES = 8
VMEM_LIMIT_BYTES = 58 * 1024 * 1024

NEG_INF = float("-inf")
NEG_BIG = -1e30


def _cparams(*sem):
    return pltpu.CompilerParams(dimension_semantics=sem, vmem_limit_bytes=VMEM_LIMIT_BYTES)


def _mm(a, b):
    return jnp.dot(a, b, preferred_element_type=F32)


def _mm_nt(a, b):
    return lax.dot_general(a, b, (((1,), (1,)), ((), ())), preferred_element_type=F32)


def _mm_tn(a, b):
    return lax.dot_general(a, b, (((0,), (0,)), ((), ())), preferred_element_type=F32)


def _layer_norm(z, g, b):
    mu = jnp.mean(z, axis=-1, keepdims=True)
    d = z - mu
    var = jnp.mean(d * d, axis=-1, keepdims=True)
    return d * lax.rsqrt(var + LN_EPS) * g + b


def _sigmoid(x):
    return 1.0 / (1.0 + jnp.exp(-x))


def _rope_tables(pos, head_dim, width):
    half = head_dim // 2
    inv = ROPE_THETA ** (-jnp.arange(0, head_dim, 2, dtype=F32) / head_dim)
    ang = pos.astype(F32)[:, None] * inv[None, :]
    cos, sin = jnp.cos(ang), jnp.sin(ang)
    cos_h = jnp.concatenate([cos, cos], axis=1)
    sin_h = jnp.concatenate([-sin, sin], axis=1)
    reps = width // head_dim
    return jnp.tile(cos_h, (1, reps)), jnp.tile(sin_h, (1, reps))


def _rope(y, cos, sin, half):
    w = y.shape[-1]
    lane = lax.broadcasted_iota(jnp.int32, y.shape, 1)
    first = (lane % (2 * half)) < half
    swapped = jnp.where(first, pltpu.roll(y, w - half, 1), pltpu.roll(y, half, 1))
    return y * cos + swapped * sin


def _proj_kernel(x_ref, cos_ref, sin_ref, *refs, groups):
    n_g = len(groups)
    w_refs, out_refs = refs[:n_g], refs[n_g:]
    xb = x_ref[...].astype(MXU_DTYPE)
    oi = 0
    for g, (rope_half, rope_w, outs) in enumerate(groups):
        y = _mm(xb, w_refs[g][...])
        if rope_half:
            yr = _rope(y[:, :rope_w], cos_ref[:, :rope_w], sin_ref[:, :rope_w], rope_half)
            y = yr if rope_w == y.shape[1] else jnp.concatenate([yr, y[:, rope_w:]], axis=1)
        for (c0, w, scale, dt) in outs:
            piece = y[:, c0:c0 + w]
            if scale != 1.0:
                piece = piece * scale
            out_refs[oi][...] = piece.astype(dt)
            oi += 1


def _token_tile(n, cap):
    tm = min(cap, n)
    assert n % tm == 0 and tm % SUBLANES == 0
    return tm


def _project(x2d, weights, groups, cos, sin, n_tab_blocks, tm):
    n = x2d.shape[0]
    out_shapes, out_specs = [], []
    for (_, _, outs) in groups:
        for (_, w, _, dt) in outs:
            out_shapes.append(jax.ShapeDtypeStruct((n, w), dt))
            out_specs.append(pl.BlockSpec((tm, w), lambda m: (m, 0)))
    tw = cos.shape[1]
    in_specs = [pl.BlockSpec((tm, D_MODEL), lambda m: (m, 0)),
                pl.BlockSpec((tm, tw), lambda m: (m % n_tab_blocks, 0)),
                pl.BlockSpec((tm, tw), lambda m: (m % n_tab_blocks, 0))]
    for w in weights:
        in_specs.append(pl.BlockSpec(w.shape, lambda m: (0, 0)))
    return pl.pallas_call(
        functools.partial(_proj_kernel, groups=groups),
        grid=(n // tm,),
        in_specs=in_specs,
        out_specs=out_specs,
        out_shape=out_shapes,
        compiler_params=_cparams("parallel"),
        name="proj",
    )(x2d, cos, sin, *weights)


def _row_count(pred):
    return jnp.sum(jnp.where(pred, 1.0, 0.0), axis=1, keepdims=True)


KEY_NEG_INF = -2139095041


def _key_to_float(key):
    bits = key ^ ((key >> 31) & jnp.int32(0x7FFFFFFF))
    return jnp.where(key <= KEY_NEG_INF, NEG_INF, lax.bitcast_convert_type(bits, F32))


def _topk_select(score, k):
    r, s = score.shape
    kf = jnp.float32(k)

    def value_bit(i, carry):
        t, cnt = carry
        cand = t + lax.shift_left(jnp.int32(1), jnp.int32(31) - i)
        c = _row_count(score >= _key_to_float(cand))
        ok = c >= kf
        return jnp.where(ok, cand, t), jnp.where(ok, c, cnt)

    t, c_ge = lax.fori_loop(0, 32, value_bit,
                            (jnp.full((r, 1), jnp.iinfo(jnp.int32).min, jnp.int32),
                             jnp.full((r, 1), float(s), F32)))
    thr = _key_to_float(t)
    gt = score > thr
    eq = score == thr
    need = kf - _row_count(gt)
    col = lax.broadcasted_iota(jnp.int32, (r, s), 1)
    n_bits = max(1, (s - 1).bit_length())

    def last_tie_column():
        def index_bit(i, j):
            cand = j + lax.shift_left(jnp.int32(1), jnp.int32(n_bits - 1) - i)
            return jnp.where(_row_count(eq & (col < cand)) < need, cand, j)
        return lax.fori_loop(0, n_bits, index_bit, jnp.zeros((r, 1), jnp.int32))

    j = lax.cond(jnp.max(c_ge) > kf, last_tie_column, lambda: jnp.full((r, 1), s, jnp.int32))
    keep = gt | (eq & (col <= j))
    return keep & (score > NEG_INF)


def _causal_widths(t_len):
    step = min(4 * Q_BLOCK, t_len)
    assert t_len % step == 0
    return tuple(range(step, t_len + 1, step))


def _for_causal_width(n_keys, widths, run, enable=True):
    prev = 0
    for w in widths:
        pl.when(enable & (n_keys > prev) & (n_keys <= w))(functools.partial(run, w))
        prev = w


def _idx_mask_kernel(qi_ref, wi_ref, kk_ref, bias_ref, *, topk, widths):
    i = pl.program_id(1)
    qb, t_len = qi_ref.shape[0], kk_ref.shape[0]
    n_keys = (i + 1) * qb
    few = n_keys <= topk

    @pl.when(few)
    def _():
        row = i * qb + lax.broadcasted_iota(jnp.int32, (qb, t_len), 0)
        col = lax.broadcasted_iota(jnp.int32, (qb, t_len), 1)
        bias_ref[...] = jnp.where(col <= row, 0.0, NEG_INF).astype(bias_ref.dtype)

    def run(w):
        qi = qi_ref[...]
        kk = kk_ref[:w, :]
        wi = wi_ref[...]
        lane = lax.broadcasted_iota(jnp.int32, (qb, LANES), 1)
        zero = jnp.zeros((qb, LANES), qi.dtype)
        score = jnp.zeros((qb, w), F32)
        for p in range(IDX_HEADS // 2):
            qp = qi[:, LANES * p:LANES * (p + 1)]
            for half in range(2):
                lhs = jnp.where(lane < IDX_DIM, qp, zero) if half == 0 else jnp.where(lane >= IDX_DIM, qp, zero)
                h = 2 * p + half
                score = score + jnp.maximum(_mm_nt(lhs, kk), 0.0) * wi[:, h:h + 1]
        row = i * qb + lax.broadcasted_iota(jnp.int32, (qb, w), 0)
        col = lax.broadcasted_iota(jnp.int32, (qb, w), 1)
        score = jnp.where(col <= row, score, NEG_INF)
        keep = _topk_select(score, topk)
        bias_ref[:, :w] = jnp.where(keep, 0.0, NEG_INF).astype(bias_ref.dtype)
        if w < t_len:
            bias_ref[:, w:] = jnp.full((qb, t_len - w), NEG_INF, bias_ref.dtype)

    _for_causal_width(n_keys, widths, run, enable=jnp.logical_not(few))


def _prompt_index_mask(qi, wi, kk, batch, t_len, topk):
    nb = t_len // Q_BLOCK
    return pl.pallas_call(
        functools.partial(_idx_mask_kernel, topk=topk, widths=_causal_widths(t_len)),
        grid=(batch, nb),
        in_specs=[pl.BlockSpec((Q_BLOCK, qi.shape[1]), lambda b, i: (b * nb + i, 0)),
                  pl.BlockSpec((Q_BLOCK, LANES), lambda b, i: (b * nb + i, 0)),
                  pl.BlockSpec((t_len, LANES), lambda b, i: (b, 0))],
        out_specs=pl.BlockSpec((Q_BLOCK, t_len), lambda b, i: (b * nb + i, 0)),
        out_shape=jax.ShapeDtypeStruct((batch * t_len, t_len), BF16),
        compiler_params=_cparams("parallel", "arbitrary"),
        name="prompt_index_mask",
    )(qi, wi, kk)


def _attn_kernel(q_ref, bias_ref, k_ref, v_ref, o_ref, *, widths):
    i = pl.program_id(1)
    qb = q_ref.shape[0]

    def run(w):
        bias = bias_ref[:, :w].astype(F32)
        bias2 = jnp.concatenate([bias, bias], axis=0)
        lane = lax.broadcasted_iota(jnp.int32, (qb, LANES), 1)
        low = lane < ATTN_HEAD_DIM
        zero = jnp.zeros((qb, LANES), q_ref.dtype)
        for p in range(ATTN_HEADS // 2):
            sl = slice(LANES * p, LANES * (p + 1))
            qp = q_ref[:, sl]
            lhs = jnp.concatenate([jnp.where(low, qp, zero), jnp.where(low, zero, qp)], axis=0)
            s = _mm_nt(lhs, k_ref[:w, sl]) + bias2
            m = jnp.max(s, axis=1, keepdims=True)
            e = jnp.exp(s - m)
            l = jnp.sum(e, axis=1, keepdims=True)
            pv = _mm(e.astype(MXU_DTYPE), v_ref[:w, sl]) / l
            o_ref[:, sl] = jnp.where(low, pv[:qb], pv[qb:]).astype(o_ref.dtype)

    _for_causal_width((i + 1) * qb, widths, run)


def _prompt_attention(q, bias, k, v, batch, t_len):
    nb = t_len // Q_BLOCK
    return pl.pallas_call(
        functools.partial(_attn_kernel, widths=_causal_widths(t_len)),
        grid=(batch, nb),
        in_specs=[pl.BlockSpec((Q_BLOCK, ATTN_WIDTH), lambda b, i: (b * nb + i, 0)),
                  pl.BlockSpec((Q_BLOCK, t_len), lambda b, i: (b * nb + i, 0)),
                  pl.BlockSpec((t_len, ATTN_WIDTH), lambda b, i: (b, 0)),
                  pl.BlockSpec((t_len, ATTN_WIDTH), lambda b, i: (b, 0))],
        out_specs=pl.BlockSpec((Q_BLOCK, ATTN_WIDTH), lambda b, i: (b * nb + i, 0)),
        out_shape=jax.ShapeDtypeStruct((batch * t_len, ATTN_WIDTH), MXU_DTYPE),
        compiler_params=_cparams("parallel", "arbitrary"),
        name="prompt_attention",
    )(q, bias, k, v)


def _sample_score_kernel(pt_ref, qi_ref, wi_ref, kin_ref, *refs, n_pages):
    del pt_ref
    page_refs, out_ref = refs[:n_pages], refs[n_pages]
    ts = qi_ref.shape[0]
    qi = qi_ref[...]
    wi = wi_ref[...]
    lhs = jnp.concatenate([qi[:, IDX_DIM * h:IDX_DIM * (h + 1)] for h in range(IDX_HEADS)],
                          axis=0).astype(MXU_DTYPE)
    wcol = jnp.concatenate([wi[:, h:h + 1] for h in range(IDX_HEADS)], axis=0)

    def head_sum(s):
        w = jnp.maximum(s, 0.0) * wcol
        acc = w[0:ts]
        for h in range(1, IDX_HEADS):
            acc = acc + w[ts * h:ts * (h + 1)]
        return acc

    for j in range(n_pages):
        kpt = page_refs[j][0].astype(MXU_DTYPE)
        out_ref[0, :, PAGE_SIZE * j:PAGE_SIZE * (j + 1)] = head_sum(_mm(lhs, kpt))
    kn = kin_ref[...].astype(MXU_DTYPE)
    kn = jnp.concatenate([kn, jnp.zeros((LANES - ts, IDX_DIM), MXU_DTYPE)], axis=0)
    sn = head_sum(_mm_nt(lhs, kn))
    row = lax.broadcasted_iota(jnp.int32, sn.shape, 0)
    col = lax.broadcasted_iota(jnp.int32, sn.shape, 1)
    past = PAGE_SIZE * n_pages
    out_ref[0, :, past:past + LANES] = jnp.where(col <= row, sn, NEG_INF)


def _sample_scores(page_table, qi, wi, ki_new, cache_ikt, dec_batch, ts):
    n_pages = page_table.shape[1]
    past = n_pages * PAGE_SIZE

    def page_spec(j):
        return pl.BlockSpec((1, IDX_DIM, PAGE_SIZE), lambda b, pt: (pt[b, j], 0, 0))

    grid_spec = pltpu.PrefetchScalarGridSpec(
        num_scalar_prefetch=1,
        grid=(dec_batch,),
        in_specs=[pl.BlockSpec((ts, qi.shape[1]), lambda b, pt: (b, 0)),
                  pl.BlockSpec((ts, LANES), lambda b, pt: (b, 0)),
                  pl.BlockSpec((ts, IDX_DIM), lambda b, pt: (b, 0))]
                 + [page_spec(j) for j in range(n_pages)],
        out_specs=pl.BlockSpec((1, ts, past + LANES), lambda b, pt: (b, 0, 0)),
    )
    return pl.pallas_call(
        functools.partial(_sample_score_kernel, n_pages=n_pages),
        grid_spec=grid_spec,
        out_shape=jax.ShapeDtypeStruct((dec_batch, ts, past + LANES), F32),
        compiler_params=_cparams("arbitrary"),
        name="sample_scores",
    )(page_table, qi, wi, ki_new, *([cache_ikt] * n_pages))


def _mask_kernel(score_ref, bias_ref, *, topk):
    keep = _topk_select(score_ref[...], topk)
    bias_ref[...] = jnp.where(keep, 0.0, NEG_INF).astype(bias_ref.dtype)


def _sample_mask(scores2d, topk):
    n, s = scores2d.shape
    rb = _token_tile(n, Q_BLOCK)
    return pl.pallas_call(
        functools.partial(_mask_kernel, topk=topk),
        grid=(n // rb,),
        in_specs=[pl.BlockSpec((rb, s), lambda i: (i, 0))],
        out_specs=pl.BlockSpec((rb, s), lambda i: (i, 0)),
        out_shape=jax.ShapeDtypeStruct((n, s), BF16),
        compiler_params=_cparams("parallel"),
        name="sample_mask",
    )(scores2d)


def _head_lane_mask(shape, h):
    lane = lax.broadcasted_iota(jnp.int32, shape, 1)
    return (lane >= ATTN_HEAD_DIM * h) & (lane < ATTN_HEAD_DIM * (h + 1))


def _sample_attn_kernel(pt_ref, q_ref, bias_ref, biasn_ref, kn_ref, vn_ref, *refs, pp):
    del pt_ref
    k_pages, v_pages = refs[:pp], refs[pp:2 * pp]
    o_ref, qbd_sc, m_sc, l_sc, acc_sc = refs[2 * pp:]
    j = pl.program_id(1)
    ts = q_ref.shape[0]

    @pl.when(j == 0)
    def _():
        q = q_ref[...]
        qbd_sc[...] = jnp.concatenate([jnp.where(_head_lane_mask(q.shape, h), q, 0.0)
                                       for h in range(ATTN_HEADS)], axis=0).astype(MXU_DTYPE)
        m_sc[...] = jnp.full(m_sc.shape, NEG_BIG, F32)
        l_sc[...] = jnp.zeros(l_sc.shape, F32)
        acc_sc[...] = jnp.zeros(acc_sc.shape, F32)

    qbd = qbd_sc[...]

    def online_update(s, pv_fn):
        m_old = m_sc[...]
        m_new = jnp.maximum(m_old, jnp.max(s, axis=1, keepdims=True))
        alpha = jnp.exp(m_old - m_new)
        p = jnp.exp(s - m_new)
        l_sc[...] = alpha * l_sc[...] + jnp.sum(p, axis=1, keepdims=True)
        acc_sc[...] = alpha * acc_sc[...] + pv_fn(p.astype(MXU_DTYPE))
        m_sc[...] = m_new

    for pg in range(pp):
        kt = k_pages[pg][0].astype(MXU_DTYPE)
        bias = bias_ref[0, :, PAGE_SIZE * pg:PAGE_SIZE * (pg + 1)].astype(F32)
        s = _mm(qbd, kt) + jnp.concatenate([bias] * ATTN_HEADS, axis=0)
        online_update(s, lambda p, pg=pg: _mm_nt(p, v_pages[pg][0].astype(MXU_DTYPE)))

    @pl.when(j == pl.num_programs(1) - 1)
    def _():
        pad = jnp.zeros((LANES - ts, ATTN_WIDTH), F32)
        kn = jnp.concatenate([kn_ref[...], pad], axis=0).astype(MXU_DTYPE)
        vn = jnp.concatenate([vn_ref[...], pad], axis=0).astype(MXU_DTYPE)
        s = _mm_nt(qbd, kn) + jnp.concatenate([biasn_ref[0].astype(F32)] * ATTN_HEADS, axis=0)
        online_update(s, lambda p: _mm(p, vn))
        out = acc_sc[...] / l_sc[...]
        o = jnp.zeros((ts, ATTN_WIDTH), F32)
        for h in range(ATTN_HEADS):
            o = o + jnp.where(_head_lane_mask(o.shape, h), out[ts * h:ts * (h + 1)], 0.0)
        o_ref[...] = o.astype(o_ref.dtype)


def _sample_attention(page_table, q, bias, k_new, v_new, cache_kt, cache_vt, dec_batch, ts, pp):
    n_pages = page_table.shape[1]
    assert n_pages % pp == 0
    past = n_pages * PAGE_SIZE
    rows = ATTN_HEADS * ts

    def page_spec(g):
        return pl.BlockSpec((1, ATTN_WIDTH, PAGE_SIZE), lambda b, j, pt: (pt[b, j * pp + g], 0, 0))

    grid_spec = pltpu.PrefetchScalarGridSpec(
        num_scalar_prefetch=1,
        grid=(dec_batch, n_pages // pp),
        in_specs=[pl.BlockSpec((ts, ATTN_WIDTH), lambda b, j, pt: (b, 0)),
                  pl.BlockSpec((1, ts, pp * PAGE_SIZE), lambda b, j, pt: (b, 0, j)),
                  pl.BlockSpec((1, ts, LANES), lambda b, j, pt: (b, 0, past // LANES)),
                  pl.BlockSpec((ts, ATTN_WIDTH), lambda b, j, pt: (b, 0)),
                  pl.BlockSpec((ts, ATTN_WIDTH), lambda b, j, pt: (b, 0))]
                 + [page_spec(g) for g in range(pp)] * 2,
        out_specs=pl.BlockSpec((ts, ATTN_WIDTH), lambda b, j, pt: (b, 0)),
        scratch_shapes=[pltpu.VMEM((rows, ATTN_WIDTH), MXU_DTYPE),
                        pltpu.VMEM((rows, 1), F32),
                        pltpu.VMEM((rows, 1), F32),
                        pltpu.VMEM((rows, ATTN_WIDTH), F32)],
    )
    return pl.pallas_call(
        functools.partial(_sample_attn_kernel, pp=pp),
        grid_spec=grid_spec,
        out_shape=jax.ShapeDtypeStruct((dec_batch * ts, ATTN_WIDTH), MXU_DTYPE),
        compiler_params=_cparams("parallel", "arbitrary"),
        name="sample_attention",
    )(page_table, q, bias, bias, k_new, v_new, *([cache_kt] * pp), *([cache_vt] * pp))


def _outproj_ln_kernel(o_ref, x_ref, w_ref, g_ref, b_ref, y_ref):
    m = _mm(o_ref[...].astype(MXU_DTYPE), w_ref[...])
    y_ref[...] = _layer_norm(DEEPNORM_ALPHA * x_ref[...] + m, g_ref[...], b_ref[...])


def _outproj_ln(o, x2d, w, g, b, tm):
    n = x2d.shape[0]
    return pl.pallas_call(
        _outproj_ln_kernel,
        grid=(n // tm,),
        in_specs=[pl.BlockSpec((tm, o.shape[1]), lambda m: (m, 0)),
                  pl.BlockSpec((tm, D_MODEL), lambda m: (m, 0)),
                  pl.BlockSpec(w.shape, lambda m: (0, 0)),
                  pl.BlockSpec((1, D_MODEL), lambda m: (0, 0)),
                  pl.BlockSpec((1, D_MODEL), lambda m: (0, 0))],
        out_specs=pl.BlockSpec((tm, D_MODEL), lambda m: (m, 0)),
        out_shape=jax.ShapeDtypeStruct((n, D_MODEL), F32),
        compiler_params=_cparams("parallel"),
        name="outproj_ln",
    )(o, x2d, w, g, b)


def _retention_tables(chunk):
    lg = jnp.log(1.0 - 2.0 ** (-5.0 - jnp.arange(RET_HEADS, dtype=F32)))
    i = jnp.arange(chunk, dtype=F32)
    diff = i[:, None] - i[None, :]
    dmask = jnp.where(diff[None] >= 0, jnp.exp(jnp.maximum(diff, 0.0)[None] * lg[:, None, None]), 0.0)
    cross = jnp.exp((i + 1.0)[:, None] * lg[None, :])
    kdec = jnp.exp((chunk - 1.0 - i)[:, None] * lg[None, :])
    cdec = jnp.exp(chunk * lg)
    pad = jnp.zeros((chunk, LANES - RET_HEADS), F32)
    cross = jnp.concatenate([cross, pad], axis=1)
    kdec = jnp.concatenate([kdec, pad], axis=1)
    cdec = jnp.broadcast_to(cdec[:, None, None], (RET_HEADS, 1, RET_V_DIM))
    return dmask.astype(F32), cross, kdec, cdec


def _retention_kernel(q_ref, k_ref, v_ref, s0_ref, dmask_ref, cross_ref, kdec_ref, cdec_ref,
                      o_ref, s_out_ref, state_sc):
    c = pl.program_id(1)

    @pl.when(c == 0)
    def _():
        state_sc[...] = s0_ref[0]

    cross = cross_ref[...]
    kdec = kdec_ref[...]
    for h in range(RET_HEADS):
        qs = slice(RET_QK_DIM * h, RET_QK_DIM * (h + 1))
        vs = slice(RET_V_DIM * h, RET_V_DIM * (h + 1))
        qh = q_ref[:, qs]
        kh = k_ref[:, qs]
        qb = qh.astype(MXU_DTYPE)
        vb = v_ref[:, vs].astype(MXU_DTYPE)
        state = state_sc[h]
        inner = _mm_nt(qb, kh.astype(MXU_DTYPE)) * dmask_ref[h]
        o = _mm(inner.astype(MXU_DTYPE), vb) + _mm(qb, state.astype(MXU_DTYPE)) * cross[:, h:h + 1]
        o_ref[:, vs] = o
        kd = (kh * kdec[:, h:h + 1]).astype(MXU_DTYPE)
        state_sc[h] = cdec_ref[h] * state + _mm_tn(kd, vb)

    @pl.when(c == pl.num_programs(1) - 1)
    def _():
        s_out_ref[0] = state_sc[...]


def _retention_scan(q, k, v, state0, batch, n_chunks, chunk):
    dmask, cross, kdec, cdec = _retention_tables(chunk)
    hq, hv = RET_HEADS * RET_QK_DIM, RET_HEADS * RET_V_DIM
    state_block = (1, RET_HEADS, RET_QK_DIM, RET_V_DIM)
    return pl.pallas_call(
        _retention_kernel,
        grid=(batch, n_chunks),
        in_specs=[pl.BlockSpec((chunk, hq), lambda b, c: (b * n_chunks + c, 0)),
                  pl.BlockSpec((chunk, hq), lambda b, c: (b * n_chunks + c, 0)),
                  pl.BlockSpec((chunk, hv), lambda b, c: (b * n_chunks + c, 0)),
                  pl.BlockSpec(state_block, lambda b, c: (b, 0, 0, 0)),
                  pl.BlockSpec(dmask.shape, lambda b, c: (0, 0, 0)),
                  pl.BlockSpec(cross.shape, lambda b, c: (0, 0)),
                  pl.BlockSpec(kdec.shape, lambda b, c: (0, 0)),
                  pl.BlockSpec(cdec.shape, lambda b, c: (0, 0, 0))],
        out_specs=[pl.BlockSpec((chunk, hv), lambda b, c: (b * n_chunks + c, 0)),
                   pl.BlockSpec(state_block, lambda b, c: (b, 0, 0, 0))],
        out_shape=[jax.ShapeDtypeStruct((batch * n_chunks * chunk, hv), F32),
                   jax.ShapeDtypeStruct((batch,) + state_block[1:], F32)],
        scratch_shapes=[pltpu.VMEM(state_block[1:], F32)],
        compiler_params=_cparams("parallel", "arbitrary"),
        name="retention_scan",
    )(q, k, v, state0, dmask, cross, kdec, cdec)


def _ret_out_kernel(o_ref, gate_ref, gain_ref, x_ref, w_ref, g_ref, b_ref, y_ref):
    pieces = []
    for h in range(RET_HEADS):
        vs = slice(RET_V_DIM * h, RET_V_DIM * (h + 1))
        of = o_ref[:, vs]
        mu = jnp.mean(of, axis=-1, keepdims=True)
        d = of - mu
        var = jnp.mean(d * d, axis=-1, keepdims=True)
        on = d * lax.rsqrt(var + LN_EPS) * gain_ref[:, vs]
        gt = gate_ref[:, vs]
        pieces.append(((gt * _sigmoid(gt)) * on).astype(MXU_DTYPE))
    m = _mm(jnp.concatenate(pieces, axis=1), w_ref[...])
    y_ref[...] = _layer_norm(DEEPNORM_ALPHA * x_ref[...] + m, g_ref[...], b_ref[...])


def _retention_output_ln(o, gate, gain, x2d, w, g, b, tm):
    n = x2d.shape[0]
    hv = RET_HEADS * RET_V_DIM
    return pl.pallas_call(
        _ret_out_kernel,
        grid=(n // tm,),
        in_specs=[pl.BlockSpec((tm, hv), lambda m: (m, 0)),
                  pl.BlockSpec((tm, hv), lambda m: (m, 0)),
                  pl.BlockSpec((1, hv), lambda m: (0, 0)),
                  pl.BlockSpec((tm, D_MODEL), lambda m: (m, 0)),
                  pl.BlockSpec(w.shape, lambda m: (0, 0)),
                  pl.BlockSpec((1, D_MODEL), lambda m: (0, 0)),
                  pl.BlockSpec((1, D_MODEL), lambda m: (0, 0))],
        out_specs=pl.BlockSpec((tm, D_MODEL), lambda m: (m, 0)),
        out_shape=jax.ShapeDtypeStruct((n, D_MODEL), F32),
        compiler_params=_cparams("parallel"),
        name="retention_out_ln",
    )(o, gate, gain, x2d, w, g, b)


def _route(logits, b_router):
    shape = logits.shape
    lane = lax.broadcasted_iota(jnp.int32, shape, 1)
    lane_f = lane.astype(F32)
    valid = lane < N_EXPERTS
    lg = jnp.where(valid, logits, NEG_INF)
    ex = jnp.exp(lg - jnp.max(lg, axis=1, keepdims=True))
    probs = ex / jnp.sum(ex, axis=1, keepdims=True)
    sel = probs + b_router
    far = jnp.float32(LANES)

    def first_max(v):
        m = jnp.max(v, axis=1, keepdims=True)
        idx = jnp.min(jnp.where(v == m, lane_f, far), axis=1, keepdims=True)
        return m, idx

    gscore = []
    for g in range(N_GROUPS):
        in_g = (lane >= EXPERTS_PER_GROUP * g) & (lane < EXPERTS_PER_GROUP * (g + 1))
        sg = jnp.where(in_g, sel, NEG_INF)
        m1, i1 = first_max(sg)
        m2 = jnp.max(jnp.where(lane_f == i1, NEG_INF, sg), axis=1, keepdims=True)
        gscore.append(m1 + m2)
    best, gbest = gscore[0], jnp.zeros_like(gscore[0])
    for g in range(1, N_GROUPS):
        better = gscore[g] > best
        best = jnp.where(better, gscore[g], best)
        gbest = jnp.where(better, jnp.float32(g), gbest)
    lo = gbest * EXPERTS_PER_GROUP
    in_best = (lane_f >= lo) & (lane_f < lo + EXPERTS_PER_GROUP)
    sm = jnp.where(in_best, sel, NEG_INF)
    _, e1 = first_max(sm)
    _, e2 = first_max(jnp.where(lane_f == e1, NEG_INF, sm))
    is1, is2 = lane_f == e1, lane_f == e2
    p1 = jnp.sum(jnp.where(is1, probs, 0.0), axis=1, keepdims=True)
    p2 = jnp.sum(jnp.where(is2, probs, 0.0), axis=1, keepdims=True)
    den = p1 + p2
    return jnp.where(is1, p1 / den, 0.0) + jnp.where(is2, p2 / den, 0.0)


def _moe_kernel(x_ref, wr_ref, br_ref, wg_ref, wu_ref, wd_ref, g_ref, b_ref, y_ref,
                xb_sc, gate_sc, acc_sc):
    e = pl.program_id(1)

    @pl.when(e == 0)
    def _():
        x = x_ref[...]
        xb_sc[...] = x.astype(MXU_DTYPE)
        logits = jnp.dot(x, wr_ref[...], preferred_element_type=F32, precision=lax.Precision.HIGHEST)
        gate_sc[...] = _route(logits, br_ref[...])
        acc_sc[...] = jnp.zeros(acc_sc.shape, F32)

    xb = xb_sc[...]
    hg = _mm(xb, wg_ref[0])
    hu = _mm(xb, wu_ref[0])
    gate = gate_sc[...]
    lane = lax.broadcasted_iota(jnp.int32, gate.shape, 1)
    ge = jnp.sum(jnp.where(lane == e, gate, 0.0), axis=1, keepdims=True)
    h = (hg * _sigmoid(hg)) * hu * ge
    acc_sc[...] += _mm(h.astype(MXU_DTYPE), wd_ref[0])

    @pl.when(e == pl.num_programs(1) - 1)
    def _():
        y_ref[...] = _layer_norm(DEEPNORM_ALPHA * x_ref[...] + acc_sc[...], g_ref[...], b_ref[...])


def _moe_ln(x2d, wr, br, wg, wu, wd, g, b, tm):
    n = x2d.shape[0]
    return pl.pallas_call(
        _moe_kernel,
        grid=(n // tm, N_EXPERTS),
        in_specs=[pl.BlockSpec((tm, D_MODEL), lambda m, e: (m, 0)),
                  pl.BlockSpec(wr.shape, lambda m, e: (0, 0)),
                  pl.BlockSpec(br.shape, lambda m, e: (0, 0)),
                  pl.BlockSpec((1, D_MODEL, EXPERT_DIM), lambda m, e: (e, 0, 0)),
                  pl.BlockSpec((1, D_MODEL, EXPERT_DIM), lambda m, e: (e, 0, 0)),
                  pl.BlockSpec((1, EXPERT_DIM, D_MODEL), lambda m, e: (e, 0, 0)),
                  pl.BlockSpec((1, D_MODEL), lambda m, e: (0, 0)),
                  pl.BlockSpec((1, D_MODEL), lambda m, e: (0, 0))],
        out_specs=pl.BlockSpec((tm, D_MODEL), lambda m, e: (m, 0)),
        out_shape=jax.ShapeDtypeStruct((n, D_MODEL), F32),
        scratch_shapes=[pltpu.VMEM((tm, D_MODEL), MXU_DTYPE),
                        pltpu.VMEM((tm, LANES), F32),
                        pltpu.VMEM((tm, D_MODEL), F32)],
        compiler_params=_cparams("parallel", "arbitrary"),
        name="moe_ln",
    )(x2d, wr, br, wg, wu, wd, g, b)


def _attn_weights(w_in):
    w = ATTN_WIDTH
    qi_w = IDX_HEADS * IDX_DIM
    wq, wk, wv = w_in[:, :w], w_in[:, w:2 * w], w_in[:, 2 * w:3 * w]
    wqi = w_in[:, 3 * w:3 * w + qi_w]
    wki = w_in[:, 3 * w + qi_w:3 * w + qi_w + IDX_DIM]
    wwi = w_in[:, 3 * w + qi_w + IDX_DIM:]
    pad = jnp.zeros((D_MODEL, LANES - IDX_HEADS), w_in.dtype)
    widx = jnp.concatenate([wqi, wki, wki, wwi, pad], axis=1)
    return [a.astype(MXU_DTYPE) for a in (wq, wk, wv, widx)]


def _attn_groups(q_dtype):
    half = ATTN_HEAD_DIM // 2
    qi_w = IDX_HEADS * IDX_DIM
    qk_scale = ATTN_HEAD_DIM ** -0.5
    idx_scale = IDX_DIM ** -0.5
    return (
        (half, ATTN_WIDTH, ((0, ATTN_WIDTH, qk_scale, q_dtype),)),
        (half, ATTN_WIDTH, ((0, ATTN_WIDTH, 1.0, F32), (0, ATTN_WIDTH, 1.0, MXU_DTYPE))),
        (0, 0, ((0, ATTN_WIDTH, 1.0, F32), (0, ATTN_WIDTH, 1.0, MXU_DTYPE))),
        (half, qi_w + LANES, ((0, qi_w, idx_scale, q_dtype),
                              (qi_w, IDX_DIM, 1.0, F32),
                              (qi_w, LANES, 1.0, MXU_DTYPE),
                              (qi_w + LANES, LANES, IDX_HEADS ** -0.5, F32))),
    )


def _ret_weights(w_in):
    hq, hv = RET_HEADS * RET_QK_DIM, RET_HEADS * RET_V_DIM
    parts = (w_in[:, :hq], w_in[:, hq:2 * hq], w_in[:, 2 * hq:2 * hq + hv], w_in[:, 2 * hq + hv:])
    return [a.astype(MXU_DTYPE) for a in parts]


def _ret_groups():
    half = RET_QK_DIM // 2
    hq, hv = RET_HEADS * RET_QK_DIM, RET_HEADS * RET_V_DIM
    return (
        (half, hq, ((0, hq, 1.0, F32),)),
        (half, hq, ((0, hq, RET_QK_DIM ** -0.5, F32),)),
        (0, 0, ((0, hv, 1.0, F32),)),
        (0, 0, ((0, hv, 1.0, F32),)),
    )


def kernel(x_prompt, x_sample, cache_k, cache_v, cache_idx_k, state_ret, page_table, w_in_attn, w_out_attn, w_in_ret, ret_norm_gain, w_out_ret, w_router, b_router, w_exp_gate, w_exp_up, w_exp_down, ln_gain, ln_bias):
    batch, t_len, _ = x_prompt.shape
    dec_batch, ts, _ = x_sample.shape
    n_pages = page_table.shape[1]
    past = n_pages * PAGE_SIZE
    pool = cache_k.shape[1]
    n_p, n_s = batch * t_len, dec_batch * ts
    topk_p = min(IDX_TOPK_MAX, t_len // 4)
    topk_s = min(IDX_TOPK_MAX, (past + ts) // 4)
    tm_p = _token_tile(t_len, 512)
    tm_s = _token_tile(n_s, 512)
    pos_p = jnp.arange(t_len, dtype=jnp.int32)
    pos_s = past + (jnp.arange(tm_s, dtype=jnp.int32) % ts)

    xp = x_prompt.reshape(n_p, D_MODEL)
    xs = x_sample.reshape(n_s, D_MODEL)

    wr = jnp.concatenate([w_router, jnp.zeros((D_MODEL, LANES - N_EXPERTS), F32)], axis=1)
    br = jnp.concatenate([b_router, jnp.zeros((LANES - N_EXPERTS,), F32)])[None, :]

    def moe(x2d, i, tm):
        return _moe_ln(x2d, wr, br, w_exp_gate[i].astype(MXU_DTYPE), w_exp_up[i].astype(MXU_DTYPE),
                       w_exp_down[i].astype(MXU_DTYPE), ln_gain[i, 1][None, :], ln_bias[i, 1][None, :], tm)

    a = 0
    aw = _attn_weights(w_in_attn[a])
    w_out = w_out_attn[a].astype(MXU_DTYPE)
    g0, b0 = ln_gain[0, 0][None, :], ln_bias[0, 0][None, :]

    cos_p, sin_p = _rope_tables(pos_p, ATTN_HEAD_DIM, ATTN_WIDTH)
    q_p, k_p, kb_p, v_p, vb_p, qi_p, ki_p, kk_p, wi_p = _project(
        xp, aw, _attn_groups(MXU_DTYPE), cos_p, sin_p, t_len // tm_p, tm_p)
    bias_p = _prompt_index_mask(qi_p, wi_p, kk_p, batch, t_len, topk_p)
    o_p = _prompt_attention(q_p, bias_p, kb_p, vb_p, batch, t_len)
    xp = _outproj_ln(o_p, xp, w_out, g0, b0, tm_p)

    cos_s, sin_s = _rope_tables(pos_s, ATTN_HEAD_DIM, ATTN_WIDTH)
    q_s, k_s, _, v_s, _, qi_s, ki_s, _, wi_s = _project(
        xs, aw, _attn_groups(F32), cos_s, sin_s, 1, tm_s)
    cache_ikt = jnp.transpose(cache_idx_k[a], (0, 2, 1))
    scores_s = _sample_scores(page_table, qi_s, wi_s, ki_s, cache_ikt, dec_batch, ts)
    bias_s = _sample_mask(scores_s.reshape(n_s, past + LANES), topk_s).reshape(dec_batch, ts, past + LANES)
    cache_kt = jnp.transpose(cache_k[a], (0, 2, 3, 1)).reshape(pool, ATTN_WIDTH, PAGE_SIZE)
    cache_vt = jnp.transpose(cache_v[a], (0, 2, 3, 1)).reshape(pool, ATTN_WIDTH, PAGE_SIZE)
    pp = max(g for g in (8, 4, 2, 1) if n_pages % g == 0)
    o_s = _sample_attention(page_table, q_s, bias_s, k_s, v_s, cache_kt, cache_vt, dec_batch, ts, pp)
    xs = _outproj_ln(o_s, xs, w_out, g0, b0, tm_s)

    new_k_p = k_p.reshape(1, batch, t_len, ATTN_HEADS, ATTN_HEAD_DIM)
    new_v_p = v_p.reshape(1, batch, t_len, ATTN_HEADS, ATTN_HEAD_DIM)
    new_ik_p = ki_p.reshape(1, batch, t_len, IDX_DIM)
    new_k_s = k_s.reshape(1, dec_batch, ts, ATTN_HEADS, ATTN_HEAD_DIM)
    new_v_s = v_s.reshape(1, dec_batch, ts, ATTN_HEADS, ATTN_HEAD_DIM)
    new_ik_s = ki_s.reshape(1, dec_batch, ts, IDX_DIM)

    xp = moe(xp, 0, _token_tile(n_p, 1024))
    xs = moe(xs, 0, _token_tile(n_s, 1024))

    r = 0
    rw = _ret_weights(w_in_ret[r])
    w_out_r = w_out_ret[r].astype(MXU_DTYPE)
    gain = ret_norm_gain[r][None, :]
    g1, b1 = ln_gain[1, 0][None, :], ln_bias[1, 0][None, :]
    hq = RET_HEADS * RET_QK_DIM
    tm_rp = _token_tile(t_len, 256)
    tm_rs = _token_tile(n_s, 256)

    cos_p, sin_p = _rope_tables(pos_p, RET_QK_DIM, hq)
    rq, rk, rv, rg = _project(xp, rw, _ret_groups(), cos_p, sin_p, t_len // tm_rp, tm_rp)
    state0 = jnp.zeros((batch, RET_HEADS, RET_QK_DIM, RET_V_DIM), F32)
    ro, state_p = _retention_scan(rq, rk, rv, state0, batch, t_len // RET_CHUNK, RET_CHUNK)
    xp = _retention_output_ln(ro, rg, gain, xp, w_out_r, g1, b1, tm_rp)

    cos_s, sin_s = _rope_tables(past + (jnp.arange(tm_rs, dtype=jnp.int32) % ts), RET_QK_DIM, hq)
    rq, rk, rv, rg = _project(xs, rw, _ret_groups(), cos_s, sin_s, 1, tm_rs)
    ro, state_s = _retention_scan(rq, rk, rv, state_ret[r], dec_batch, 1, ts)
    xs = _retention_output_ln(ro, rg, gain, xs, w_out_r, g1, b1, tm_rs)

    xp = moe(xp, 1, _token_tile(n_p, 1024))
    xs = moe(xs, 1, _token_tile(n_s, 1024))

    return (xp.reshape(batch, t_len, D_MODEL), xs.reshape(dec_batch, ts, D_MODEL),
            new_k_p, new_v_p, new_ik_p, new_k_s, new_v_s, new_ik_s,
            state_p[None], state_s[None])
```

```python
import functools

import jax
import jax.numpy as jnp
from jax import lax
from jax.experimental import pallas as pl
from jax.experimental.pallas import tpu as pltpu

F32 = jnp.float32
BF16 = jnp.bfloat16
MXU_DTYPE = jnp.bfloat16

D_MODEL = 1024
PAGE_SIZE = 128
ATTN_HEADS = 16
ATTN_HEAD_DIM = 64
ATTN_WIDTH = ATTN_HEADS * ATTN_HEAD_DIM
IDX_HEADS = 8
IDX_DIM = 64
IDX_TOPK_MAX = 256
Q_BLOCK = 128
ROPE_THETA = 10000.0
RET_HEADS = 4
RET_QK_DIM = D_MODEL // RET_HEADS
RET_V_DIM = 2 * D_MODEL // RET_HEADS
RET_CHUNK = 128
N_EXPERTS = 16
N_GROUPS = 4
EXPERTS_PER_GROUP = N_EXPERTS // N_GROUPS
EXPERT_DIM = D_MODEL // 2
DEPTH = 2
DEEPNORM_ALPHA = (2 * DEPTH) ** 0.25
LN_EPS = 1e-5

LANES = 128
SUBLANES = 8
VMEM_LIMIT_BYTES = 58 * 1024 * 1024

NEG_INF = float("-inf")


def _cparams(*sem):
    return pltpu.CompilerParams(dimension_semantics=sem, vmem_limit_bytes=VMEM_LIMIT_BYTES)


def _mm(a, b):
    return jnp.dot(a, b, preferred_element_type=F32)


def _mm_nt(a, b):
    return lax.dot_general(a, b, (((1,), (1,)), ((), ())), preferred_element_type=F32)


def _mm_tn(a, b):
    return lax.dot_general(a, b, (((0,), (0,)), ((), ())), preferred_element_type=F32)


def _layer_norm(z, g, b):
    mu = jnp.mean(z, axis=-1, keepdims=True)
    d = z - mu
    var = jnp.mean(d * d, axis=-1, keepdims=True)
    return d * lax.rsqrt(var + LN_EPS) * g + b


def _sigmoid(x):
    return 1.0 / (1.0 + jnp.exp(-x))


def _rope_tables(pos, head_dim, width):
    half = head_dim // 2
    inv = ROPE_THETA ** (-jnp.arange(0, head_dim, 2, dtype=F32) / head_dim)
    ang = pos.astype(F32)[:, None] * inv[None, :]
    cos, sin = jnp.cos(ang), jnp.sin(ang)
    cos_h = jnp.concatenate([cos, cos], axis=1)
    sin_h = jnp.concatenate([-sin, sin], axis=1)
    reps = width // head_dim
    return jnp.tile(cos_h, (1, reps)), jnp.tile(sin_h, (1, reps))


def _rope(y, cos, sin, half):
    w = y.shape[-1]
    lane = lax.broadcasted_iota(jnp.int32, y.shape, 1)
    first = (lane % (2 * half)) < half
    swapped = jnp.where(first, pltpu.roll(y, w - half, 1), pltpu.roll(y, half, 1))
    return y * cos + swapped * sin


def _proj_kernel(x_ref, cos_ref, sin_ref, *refs, groups):
    n_g = len(groups)
    w_refs, out_refs = refs[:n_g], refs[n_g:]
    xb = x_ref[...].astype(MXU_DTYPE)
    oi = 0
    for g, (rope_half, rope_w, outs) in enumerate(groups):
        y = _mm(xb, w_refs[g][...])
        if rope_half:
            yr = _rope(y[:, :rope_w], cos_ref[:, :rope_w], sin_ref[:, :rope_w], rope_half)
            y = yr if rope_w == y.shape[1] else jnp.concatenate([yr, y[:, rope_w:]], axis=1)
        for (c0, w, scale, dt) in outs:
            piece = y[:, c0:c0 + w]
            if scale != 1.0:
                piece = piece * scale
            out_refs[oi][...] = piece.astype(dt)
            oi += 1


def _token_tile(n, cap):
    tm = min(cap, n)
    assert n % tm == 0 and tm % SUBLANES == 0
    return tm


def _project(x2d, weights, groups, cos, sin, n_tab_blocks, tm):
    n = x2d.shape[0]
    out_shapes, out_specs = [], []
    for (_, _, outs) in groups:
        for (_, w, _, dt) in outs:
            out_shapes.append(jax.ShapeDtypeStruct((n, w), dt))
            out_specs.append(pl.BlockSpec((tm, w), lambda m: (m, 0)))
    tw = cos.shape[1]
    in_specs = [pl.BlockSpec((tm, D_MODEL), lambda m: (m, 0)),
                pl.BlockSpec((tm, tw), lambda m: (m % n_tab_blocks, 0)),
                pl.BlockSpec((tm, tw), lambda m: (m % n_tab_blocks, 0))]
    for w in weights:
        in_specs.append(pl.BlockSpec(w.shape, lambda m: (0, 0)))
    return pl.pallas_call(
        functools.partial(_proj_kernel, groups=groups),
        grid=(n // tm,),
        in_specs=in_specs,
        out_specs=out_specs,
        out_shape=out_shapes,
        compiler_params=_cparams("parallel"),
        name="proj",
    )(x2d, cos, sin, *weights)


def _row_count(pred):
    return jnp.sum(jnp.where(pred, 1.0, 0.0), axis=1, keepdims=True)


KEY_NEG_INF = -2139095041


def _key_to_float(key):
    bits = key ^ ((key >> 31) & jnp.int32(0x7FFFFFFF))
    return jnp.where(key <= KEY_NEG_INF, NEG_INF, lax.bitcast_convert_type(bits, F32))


def _topk_select(score, k):
    r, s = score.shape
    kf = jnp.float32(k)

    def value_bit(i, carry):
        t, cnt = carry
        cand = t + lax.shift_left(jnp.int32(1), jnp.int32(31) - i)
        c = _row_count(score >= _key_to_float(cand))
        ok = c >= kf
        return jnp.where(ok, cand, t), jnp.where(ok, c, cnt)

    t, c_ge = lax.fori_loop(0, 32, value_bit,
                            (jnp.full((r, 1), jnp.iinfo(jnp.int32).min, jnp.int32),
                             jnp.full((r, 1), float(s), F32)))
    thr = _key_to_float(t)
    gt = score > thr
    eq = score == thr
    need = kf - _row_count(gt)
    col = lax.broadcasted_iota(jnp.int32, (r, s), 1)
    n_bits = max(1, (s - 1).bit_length())

    def last_tie_column():
        def index_bit(i, j):
            cand = j + lax.shift_left(jnp.int32(1), jnp.int32(n_bits - 1) - i)
            return jnp.where(_row_count(eq & (col < cand)) < need, cand, j)
        return lax.fori_loop(0, n_bits, index_bit, jnp.zeros((r, 1), jnp.int32))

    j = lax.cond(jnp.max(c_ge) > kf, last_tie_column, lambda: jnp.full((r, 1), s, jnp.int32))
    keep = gt | (eq & (col <= j))
    return keep & (score > NEG_INF)


def _causal_widths(t_len):
    step = min(2 * Q_BLOCK, t_len)
    assert t_len % step == 0
    return tuple(range(step, t_len + 1, step))


def _for_causal_width(n_keys, widths, run, enable=True):
    prev = 0
    for w in widths:
        pl.when(enable & (n_keys > prev) & (n_keys <= w))(functools.partial(run, w))
        prev = w


def _idx_mask_kernel(qi_ref, wi_ref, kk_ref, bias_ref, *, topk, widths):
    i = pl.program_id(1)
    qb, t_len = qi_ref.shape[0], kk_ref.shape[0]
    n_keys = (i + 1) * qb
    few = n_keys <= topk

    @pl.when(few)
    def _():
        row = i * qb + lax.broadcasted_iota(jnp.int32, (qb, t_len), 0)
        col = lax.broadcasted_iota(jnp.int32, (qb, t_len), 1)
        bias_ref[...] = jnp.where(col <= row, 0.0, NEG_INF).astype(bias_ref.dtype)

    def run(w):
        qi = qi_ref[...]
        kk = kk_ref[:w, :]
        wi = wi_ref[...]
        lane = lax.broadcasted_iota(jnp.int32, (qb, LANES), 1)
        zero = jnp.zeros((qb, LANES), qi.dtype)
        score = jnp.zeros((qb, w), F32)
        for p in range(IDX_HEADS // 2):
            qp = qi[:, LANES * p:LANES * (p + 1)]
            for half in range(2):
                lhs = jnp.where(lane < IDX_DIM, qp, zero) if half == 0 else jnp.where(lane >= IDX_DIM, qp, zero)
                h = 2 * p + half
                score = score + jnp.maximum(_mm_nt(lhs, kk), 0.0) * wi[:, h:h + 1]
        row = i * qb + lax.broadcasted_iota(jnp.int32, (qb, w), 0)
        col = lax.broadcasted_iota(jnp.int32, (qb, w), 1)
        score = jnp.where(col <= row, score, NEG_INF)
        keep = _topk_select(score, topk)
        bias_ref[:, :w] = jnp.where(keep, 0.0, NEG_INF).astype(bias_ref.dtype)
        if w < t_len:
            bias_ref[:, w:] = jnp.full((qb, t_len - w), NEG_INF, bias_ref.dtype)

    _for_causal_width(n_keys, widths, run, enable=jnp.logical_not(few))


def _prompt_index_mask(qi, wi, kk, batch, t_len, topk):
    nb = t_len // Q_BLOCK
    return pl.pallas_call(
        functools.partial(_idx_mask_kernel, topk=topk, widths=_causal_widths(t_len)),
        grid=(batch, nb),
        in_specs=[pl.BlockSpec((Q_BLOCK, qi.shape[1]), lambda b, i: (b * nb + i, 0)),
                  pl.BlockSpec((Q_BLOCK, LANES), lambda b, i: (b * nb + i, 0)),
                  pl.BlockSpec((t_len, LANES), lambda b, i: (b, 0))],
        out_specs=pl.BlockSpec((Q_BLOCK, t_len), lambda b, i: (b * nb + i, 0)),
        out_shape=jax.ShapeDtypeStruct((batch * t_len, t_len), BF16),
        compiler_params=_cparams("parallel", "arbitrary"),
        name="prompt_index_mask",
    )(qi, wi, kk)


def _attn_kernel(q_ref, bias_ref, k_ref, v_ref, o_ref, *, widths):
    i = pl.program_id(1)
    qb = q_ref.shape[0]

    def run(w):
        bias = bias_ref[:, :w].astype(F32)
        bias2 = jnp.concatenate([bias, bias], axis=0)
        lane = lax.broadcasted_iota(jnp.int32, (qb, LANES), 1)
        low = lane < ATTN_HEAD_DIM
        zero = jnp.zeros((qb, LANES), q_ref.dtype)
        for p in range(ATTN_HEADS // 2):
            sl = slice(LANES * p, LANES * (p + 1))
            qp = q_ref[:, sl]
            lhs = jnp.concatenate([jnp.where(low, qp, zero), jnp.where(low, zero, qp)], axis=0)
            s = _mm_nt(lhs, k_ref[:w, sl]) + bias2
            m = jnp.max(s, axis=1, keepdims=True)
            e = jnp.exp(s - m)
            l = jnp.sum(e, axis=1, keepdims=True)
            pv = _mm(e.astype(MXU_DTYPE), v_ref[:w, sl]) / l
            o_ref[:, sl] = jnp.where(low, pv[:qb], pv[qb:]).astype(o_ref.dtype)

    _for_causal_width((i + 1) * qb, widths, run)


def _prompt_attention(q, bias, k, v, batch, t_len):
    nb = t_len // Q_BLOCK
    return pl.pallas_call(
        functools.partial(_attn_kernel, widths=_causal_widths(t_len)),
        grid=(batch, nb),
        in_specs=[pl.BlockSpec((Q_BLOCK, ATTN_WIDTH), lambda b, i: (b * nb + i, 0)),
                  pl.BlockSpec((Q_BLOCK, t_len), lambda b, i: (b * nb + i, 0)),
                  pl.BlockSpec((t_len, ATTN_WIDTH), lambda b, i: (b, 0)),
                  pl.BlockSpec((t_len, ATTN_WIDTH), lambda b, i: (b, 0))],
        out_specs=pl.BlockSpec((Q_BLOCK, ATTN_WIDTH), lambda b, i: (b * nb + i, 0)),
        out_shape=jax.ShapeDtypeStruct((batch * t_len, ATTN_WIDTH), MXU_DTYPE),
        compiler_params=_cparams("parallel", "arbitrary"),
        name="prompt_attention",
    )(q, bias, k, v)


def _sample_score_kernel(pt_ref, qi_ref, wi_ref, kin_ref, *refs, n_pages):
    del pt_ref
    page_refs, out_ref = refs[:n_pages], refs[n_pages]
    ts = qi_ref.shape[0]
    qi = qi_ref[...]
    wi = wi_ref[...]
    lhs = jnp.concatenate([qi[:, IDX_DIM * h:IDX_DIM * (h + 1)] for h in range(IDX_HEADS)],
                          axis=0).astype(MXU_DTYPE)
    wcol = jnp.concatenate([wi[:, h:h + 1] for h in range(IDX_HEADS)], axis=0)

    def head_sum(s):
        w = jnp.maximum(s, 0.0) * wcol
        acc = w[0:ts]
        for h in range(1, IDX_HEADS):
            acc = acc + w[ts * h:ts * (h + 1)]
        return acc

    for j in range(n_pages):
        kpt = page_refs[j][0].astype(MXU_DTYPE)
        out_ref[0, :, PAGE_SIZE * j:PAGE_SIZE * (j + 1)] = head_sum(_mm(lhs, kpt))
    kn = kin_ref[...].astype(MXU_DTYPE)
    kn = jnp.concatenate([kn, jnp.zeros((LANES - ts, IDX_DIM), MXU_DTYPE)], axis=0)
    sn = head_sum(_mm_nt(lhs, kn))
    row = lax.broadcasted_iota(jnp.int32, sn.shape, 0)
    col = lax.broadcasted_iota(jnp.int32, sn.shape, 1)
    past = PAGE_SIZE * n_pages
    out_ref[0, :, past:past + LANES] = jnp.where(col <= row, sn, NEG_INF)


def _sample_scores(page_table, qi, wi, ki_new, cache_ikt, dec_batch, ts):
    n_pages = page_table.shape[1]
    past = n_pages * PAGE_SIZE

    def page_spec(j):
        return pl.BlockSpec((1, IDX_DIM, PAGE_SIZE), lambda b, pt: (pt[b, j], 0, 0))

    grid_spec = pltpu.PrefetchScalarGridSpec(
        num_scalar_prefetch=1,
        grid=(dec_batch,),
        in_specs=[pl.BlockSpec((ts, qi.shape[1]), lambda b, pt: (b, 0)),
                  pl.BlockSpec((ts, LANES), lambda b, pt: (b, 0)),
                  pl.BlockSpec((ts, IDX_DIM), lambda b, pt: (b, 0))]
                 + [page_spec(j) for j in range(n_pages)],
        out_specs=pl.BlockSpec((1, ts, past + LANES), lambda b, pt: (b, 0, 0)),
    )
    return pl.pallas_call(
        functools.partial(_sample_score_kernel, n_pages=n_pages),
        grid_spec=grid_spec,
        out_shape=jax.ShapeDtypeStruct((dec_batch, ts, past + LANES), F32),
        compiler_params=_cparams("arbitrary"),
        name="sample_scores",
    )(page_table, qi, wi, ki_new, *([cache_ikt] * n_pages))


def _mask_kernel(score_ref, bias_ref, *, topk):
    keep = _topk_select(score_ref[...], topk)
    bias_ref[...] = jnp.where(keep, 0.0, NEG_INF).astype(bias_ref.dtype)


def _sample_mask(scores2d, topk):
    n, s = scores2d.shape
    rb = _token_tile(n, Q_BLOCK)
    return pl.pallas_call(
        functools.partial(_mask_kernel, topk=topk),
        grid=(n // rb,),
        in_specs=[pl.BlockSpec((rb, s), lambda i: (i, 0))],
        out_specs=pl.BlockSpec((rb, s), lambda i: (i, 0)),
        out_shape=jax.ShapeDtypeStruct((n, s), BF16),
        compiler_params=_cparams("parallel"),
        name="sample_mask",
    )(scores2d)


def _head_lane_mask(shape, h):
    lane = lax.broadcasted_iota(jnp.int32, shape, 1)
    return (lane >= ATTN_HEAD_DIM * h) & (lane < ATTN_HEAD_DIM * (h + 1))


def _sample_attn_kernel(pt_ref, q_ref, bias_ref, kn_ref, vn_ref, *refs, n_pages):
    del pt_ref
    k_pages, v_pages, o_ref = refs[:n_pages], refs[n_pages:2 * n_pages], refs[2 * n_pages]
    ts = q_ref.shape[0]
    past = n_pages * PAGE_SIZE
    q = q_ref[...]
    qbd = jnp.concatenate([jnp.where(_head_lane_mask(q.shape, h), q, 0.0)
                           for h in range(ATTN_HEADS)], axis=0).astype(MXU_DTYPE)
    pad = jnp.zeros((LANES - ts, ATTN_WIDTH), F32)
    kn = jnp.concatenate([kn_ref[...], pad], axis=0).astype(MXU_DTYPE)
    vn = jnp.concatenate([vn_ref[...], pad], axis=0).astype(MXU_DTYPE)
    s = jnp.concatenate([_mm(qbd, k_pages[pg][0].astype(MXU_DTYPE)) for pg in range(n_pages)]
                        + [_mm_nt(qbd, kn)], axis=1)
    s = s + jnp.concatenate([bias_ref[0].astype(F32)] * ATTN_HEADS, axis=0)
    m = jnp.max(s, axis=1, keepdims=True)
    e = jnp.exp(s - m)
    l = jnp.sum(e, axis=1, keepdims=True)
    eb = e.astype(MXU_DTYPE)
    pv = _mm(eb[:, past:], vn)
    for pg in range(n_pages):
        pv = pv + _mm_nt(eb[:, PAGE_SIZE * pg:PAGE_SIZE * (pg + 1)], v_pages[pg][0].astype(MXU_DTYPE))
    out = pv / l
    o = jnp.zeros((ts, ATTN_WIDTH), F32)
    for h in range(ATTN_HEADS):
        o = o + jnp.where(_head_lane_mask(o.shape, h), out[ts * h:ts * (h + 1)], 0.0)
    o_ref[...] = o.astype(o_ref.dtype)


def _sample_attention(page_table, q, bias, k_new, v_new, cache_kt, cache_vt, dec_batch, ts):
    n_pages = page_table.shape[1]
    past = n_pages * PAGE_SIZE

    def page_spec(g):
        return pl.BlockSpec((1, ATTN_WIDTH, PAGE_SIZE), lambda b, pt: (pt[b, g], 0, 0))

    grid_spec = pltpu.PrefetchScalarGridSpec(
        num_scalar_prefetch=1,
        grid=(dec_batch,),
        in_specs=[pl.BlockSpec((ts, ATTN_WIDTH), lambda b, pt: (b, 0)),
                  pl.BlockSpec((1, ts, past + LANES), lambda b, pt: (b, 0, 0)),
                  pl.BlockSpec((ts, ATTN_WIDTH), lambda b, pt: (b, 0)),
                  pl.BlockSpec((ts, ATTN_WIDTH), lambda b, pt: (b, 0))]
                 + [page_spec(g) for g in range(n_pages)] * 2,
        out_specs=pl.BlockSpec((ts, ATTN_WIDTH), lambda b, pt: (b, 0)),
    )
    return pl.pallas_call(
        functools.partial(_sample_attn_kernel, n_pages=n_pages),
        grid_spec=grid_spec,
        out_shape=jax.ShapeDtypeStruct((dec_batch * ts, ATTN_WIDTH), MXU_DTYPE),
        compiler_params=_cparams("parallel"),
        name="sample_attention",
    )(page_table, q, bias, k_new, v_new, *([cache_kt] * n_pages), *([cache_vt] * n_pages))


def _outproj_ln_kernel(o_ref, x_ref, w_ref, g_ref, b_ref, y_ref):
    m = _mm(o_ref[...].astype(MXU_DTYPE), w_ref[...])
    y_ref[...] = _layer_norm(DEEPNORM_ALPHA * x_ref[...] + m, g_ref[...], b_ref[...])


def _outproj_ln(o, x2d, w, g, b, tm):
    n = x2d.shape[0]
    return pl.pallas_call(
        _outproj_ln_kernel,
        grid=(n // tm,),
        in_specs=[pl.BlockSpec((tm, o.shape[1]), lambda m: (m, 0)),
                  pl.BlockSpec((tm, D_MODEL), lambda m: (m, 0)),
                  pl.BlockSpec(w.shape, lambda m: (0, 0)),
                  pl.BlockSpec((1, D_MODEL), lambda m: (0, 0)),
                  pl.BlockSpec((1, D_MODEL), lambda m: (0, 0))],
        out_specs=pl.BlockSpec((tm, D_MODEL), lambda m: (m, 0)),
        out_shape=jax.ShapeDtypeStruct((n, D_MODEL), F32),
        compiler_params=_cparams("parallel"),
        name="outproj_ln",
    )(o, x2d, w, g, b)


def _retention_tables(chunk):
    lg = jnp.log(1.0 - 2.0 ** (-5.0 - jnp.arange(RET_HEADS, dtype=F32)))
    i = jnp.arange(chunk, dtype=F32)
    diff = i[:, None] - i[None, :]
    dmask = jnp.where(diff[None] >= 0, jnp.exp(jnp.maximum(diff, 0.0)[None] * lg[:, None, None]), 0.0)
    cross = jnp.exp((i + 1.0)[:, None] * lg[None, :])
    kdec = jnp.exp((chunk - 1.0 - i)[:, None] * lg[None, :])
    cdec = jnp.exp(chunk * lg)
    pad = jnp.zeros((chunk, LANES - RET_HEADS), F32)
    cross = jnp.concatenate([cross, pad], axis=1)
    kdec = jnp.concatenate([kdec, pad], axis=1)
    cdec = jnp.broadcast_to(cdec[:, None, None], (RET_HEADS, 1, RET_V_DIM))
    return dmask.astype(F32), cross, kdec, cdec


def _retention_kernel(q_ref, k_ref, v_ref, s0_ref, dmask_ref, cross_ref, kdec_ref, cdec_ref,
                      o_ref, s_out_ref, state_sc):
    c = pl.program_id(1)

    @pl.when(c == 0)
    def _():
        state_sc[...] = s0_ref[0]

    cross = cross_ref[...]
    kdec = kdec_ref[...]
    for h in range(RET_HEADS):
        qs = slice(RET_QK_DIM * h, RET_QK_DIM * (h + 1))
        vs = slice(RET_V_DIM * h, RET_V_DIM * (h + 1))
        qh = q_ref[:, qs]
        kh = k_ref[:, qs]
        qb = qh.astype(MXU_DTYPE)
        vb = v_ref[:, vs].astype(MXU_DTYPE)
        state = state_sc[h]
        inner = _mm_nt(qb, kh.astype(MXU_DTYPE)) * dmask_ref[h]
        o = _mm(inner.astype(MXU_DTYPE), vb) + _mm(qb, state.astype(MXU_DTYPE)) * cross[:, h:h + 1]
        o_ref[:, vs] = o
        kd = (kh * kdec[:, h:h + 1]).astype(MXU_DTYPE)
        state_sc[h] = cdec_ref[h] * state + _mm_tn(kd, vb)

    @pl.when(c == pl.num_programs(1) - 1)
    def _():
        s_out_ref[0] = state_sc[...]


def _retention_scan(q, k, v, state0, batch, n_chunks, chunk):
    dmask, cross, kdec, cdec = _retention_tables(chunk)
    hq, hv = RET_HEADS * RET_QK_DIM, RET_HEADS * RET_V_DIM
    state_block = (1, RET_HEADS, RET_QK_DIM, RET_V_DIM)
    return pl.pallas_call(
        _retention_kernel,
        grid=(batch, n_chunks),
        in_specs=[pl.BlockSpec((chunk, hq), lambda b, c: (b * n_chunks + c, 0)),
                  pl.BlockSpec((chunk, hq), lambda b, c: (b * n_chunks + c, 0)),
                  pl.BlockSpec((chunk, hv), lambda b, c: (b * n_chunks + c, 0)),
                  pl.BlockSpec(state_block, lambda b, c: (b, 0, 0, 0)),
                  pl.BlockSpec(dmask.shape, lambda b, c: (0, 0, 0)),
                  pl.BlockSpec(cross.shape, lambda b, c: (0, 0)),
                  pl.BlockSpec(kdec.shape, lambda b, c: (0, 0)),
                  pl.BlockSpec(cdec.shape, lambda b, c: (0, 0, 0))],
        out_specs=[pl.BlockSpec((chunk, hv), lambda b, c: (b * n_chunks + c, 0)),
                   pl.BlockSpec(state_block, lambda b, c: (b, 0, 0, 0))],
        out_shape=[jax.ShapeDtypeStruct((batch * n_chunks * chunk, hv), F32),
                   jax.ShapeDtypeStruct((batch,) + state_block[1:], F32)],
        scratch_shapes=[pltpu.VMEM(state_block[1:], F32)],
        compiler_params=_cparams("parallel", "arbitrary"),
        name="retention_scan",
    )(q, k, v, state0, dmask, cross, kdec, cdec)


def _ret_out_kernel(o_ref, gate_ref, gain_ref, x_ref, w_ref, g_ref, b_ref, y_ref):
    pieces = []
    for h in range(RET_HEADS):
        vs = slice(RET_V_DIM * h, RET_V_DIM * (h + 1))
        of = o_ref[:, vs]
        mu = jnp.mean(of, axis=-1, keepdims=True)
        d = of - mu
        var = jnp.mean(d * d, axis=-1, keepdims=True)
        on = d * lax.rsqrt(var + LN_EPS) * gain_ref[:, vs]
        gt = gate_ref[:, vs]
        pieces.append(((gt * _sigmoid(gt)) * on).astype(MXU_DTYPE))
    m = _mm(jnp.concatenate(pieces, axis=1), w_ref[...])
    y_ref[...] = _layer_norm(DEEPNORM_ALPHA * x_ref[...] + m, g_ref[...], b_ref[...])


def _retention_output_ln(o, gate, gain, x2d, w, g, b, tm):
    n = x2d.shape[0]
    hv = RET_HEADS * RET_V_DIM
    return pl.pallas_call(
        _ret_out_kernel,
        grid=(n // tm,),
        in_specs=[pl.BlockSpec((tm, hv), lambda m: (m, 0)),
                  pl.BlockSpec((tm, hv), lambda m: (m, 0)),
                  pl.BlockSpec((1, hv), lambda m: (0, 0)),
                  pl.BlockSpec((tm, D_MODEL), lambda m: (m, 0)),
                  pl.BlockSpec(w.shape, lambda m: (0, 0)),
                  pl.BlockSpec((1, D_MODEL), lambda m: (0, 0)),
                  pl.BlockSpec((1, D_MODEL), lambda m: (0, 0))],
        out_specs=pl.BlockSpec((tm, D_MODEL), lambda m: (m, 0)),
        out_shape=jax.ShapeDtypeStruct((n, D_MODEL), F32),
        compiler_params=_cparams("parallel"),
        name="retention_out_ln",
    )(o, gate, gain, x2d, w, g, b)


def _route(logits, b_router):
    shape = logits.shape
    lane = lax.broadcasted_iota(jnp.int32, shape, 1)
    lane_f = lane.astype(F32)
    valid = lane < N_EXPERTS
    lg = jnp.where(valid, logits, NEG_INF)
    ex = jnp.exp(lg - jnp.max(lg, axis=1, keepdims=True))
    probs = ex / jnp.sum(ex, axis=1, keepdims=True)
    sel = probs + b_router
    far = jnp.float32(LANES)

    def first_max(v):
        m = jnp.max(v, axis=1, keepdims=True)
        idx = jnp.min(jnp.where(v == m, lane_f, far), axis=1, keepdims=True)
        return m, idx

    gscore = []
    for g in range(N_GROUPS):
        in_g = (lane >= EXPERTS_PER_GROUP * g) & (lane < EXPERTS_PER_GROUP * (g + 1))
        sg = jnp.where(in_g, sel, NEG_INF)
        m1, i1 = first_max(sg)
        m2 = jnp.max(jnp.where(lane_f == i1, NEG_INF, sg), axis=1, keepdims=True)
        gscore.append(m1 + m2)
    best, gbest = gscore[0], jnp.zeros_like(gscore[0])
    for g in range(1, N_GROUPS):
        better = gscore[g] > best
        best = jnp.where(better, gscore[g], best)
        gbest = jnp.where(better, jnp.float32(g), gbest)
    lo = gbest * EXPERTS_PER_GROUP
    in_best = (lane_f >= lo) & (lane_f < lo + EXPERTS_PER_GROUP)
    sm = jnp.where(in_best, sel, NEG_INF)
    _, e1 = first_max(sm)
    _, e2 = first_max(jnp.where(lane_f == e1, NEG_INF, sm))
    is1, is2 = lane_f == e1, lane_f == e2
    p1 = jnp.sum(jnp.where(is1, probs, 0.0), axis=1, keepdims=True)
    p2 = jnp.sum(jnp.where(is2, probs, 0.0), axis=1, keepdims=True)
    den = p1 + p2
    return jnp.where(is1, p1 / den, 0.0) + jnp.where(is2, p2 / den, 0.0)


def _moe_kernel(x_ref, wr_ref, br_ref, wg_ref, wu_ref, wd_ref, g_ref, b_ref, y_ref,
                xb_sc, gate_sc, acc_sc):
    e = pl.program_id(1)

    @pl.when(e == 0)
    def _():
        x = x_ref[...]
        xb_sc[...] = x.astype(MXU_DTYPE)
        logits = jnp.dot(x, wr_ref[...], preferred_element_type=F32, precision=lax.Precision.HIGHEST)
        gate_sc[...] = _route(logits, br_ref[...])
        acc_sc[...] = jnp.zeros(acc_sc.shape, F32)

    xb = xb_sc[...]
    hg = _mm(xb, wg_ref[0])
    hu = _mm(xb, wu_ref[0])
    gate = gate_sc[...]
    lane = lax.broadcasted_iota(jnp.int32, gate.shape, 1)
    ge = jnp.sum(jnp.where(lane == e, gate, 0.0), axis=1, keepdims=True)
    h = (hg * _sigmoid(hg)) * hu * ge
    acc_sc[...] += _mm(h.astype(MXU_DTYPE), wd_ref[0])

    @pl.when(e == pl.num_programs(1) - 1)
    def _():
        y_ref[...] = _layer_norm(DEEPNORM_ALPHA * x_ref[...] + acc_sc[...], g_ref[...], b_ref[...])


def _moe_ln(x2d, wr, br, wg, wu, wd, g, b, tm):
    n = x2d.shape[0]
    return pl.pallas_call(
        _moe_kernel,
        grid=(n // tm, N_EXPERTS),
        in_specs=[pl.BlockSpec((tm, D_MODEL), lambda m, e: (m, 0)),
                  pl.BlockSpec(wr.shape, lambda m, e: (0, 0)),
                  pl.BlockSpec(br.shape, lambda m, e: (0, 0)),
                  pl.BlockSpec((1, D_MODEL, EXPERT_DIM), lambda m, e: (e, 0, 0)),
                  pl.BlockSpec((1, D_MODEL, EXPERT_DIM), lambda m, e: (e, 0, 0)),
                  pl.BlockSpec((1, EXPERT_DIM, D_MODEL), lambda m, e: (e, 0, 0)),
                  pl.BlockSpec((1, D_MODEL), lambda m, e: (0, 0)),
                  pl.BlockSpec((1, D_MODEL), lambda m, e: (0, 0))],
        out_specs=pl.BlockSpec((tm, D_MODEL), lambda m, e: (m, 0)),
        out_shape=jax.ShapeDtypeStruct((n, D_MODEL), F32),
        scratch_shapes=[pltpu.VMEM((tm, D_MODEL), MXU_DTYPE),
                        pltpu.VMEM((tm, LANES), F32),
                        pltpu.VMEM((tm, D_MODEL), F32)],
        compiler_params=_cparams("parallel", "arbitrary"),
        name="moe_ln",
    )(x2d, wr, br, wg, wu, wd, g, b)


def _attn_weights(w_in):
    w = ATTN_WIDTH
    qi_w = IDX_HEADS * IDX_DIM
    wq, wk, wv = w_in[:, :w], w_in[:, w:2 * w], w_in[:, 2 * w:3 * w]
    wqi = w_in[:, 3 * w:3 * w + qi_w]
    wki = w_in[:, 3 * w + qi_w:3 * w + qi_w + IDX_DIM]
    wwi = w_in[:, 3 * w + qi_w + IDX_DIM:]
    pad = jnp.zeros((D_MODEL, LANES - IDX_HEADS), w_in.dtype)
    widx = jnp.concatenate([wqi, wki, wki, wwi, pad], axis=1)
    return [a.astype(MXU_DTYPE) for a in (wq, wk, wv, widx)]


def _attn_groups(q_dtype):
    half = ATTN_HEAD_DIM // 2
    qi_w = IDX_HEADS * IDX_DIM
    qk_scale = ATTN_HEAD_DIM ** -0.5
    idx_scale = IDX_DIM ** -0.5
    return (
        (half, ATTN_WIDTH, ((0, ATTN_WIDTH, qk_scale, q_dtype),)),
        (half, ATTN_WIDTH, ((0, ATTN_WIDTH, 1.0, F32), (0, ATTN_WIDTH, 1.0, MXU_DTYPE))),
        (0, 0, ((0, ATTN_WIDTH, 1.0, F32), (0, ATTN_WIDTH, 1.0, MXU_DTYPE))),
        (half, qi_w + LANES, ((0, qi_w, idx_scale, q_dtype),
                              (qi_w, IDX_DIM, 1.0, F32),
                              (qi_w, LANES, 1.0, MXU_DTYPE),
                              (qi_w + LANES, LANES, IDX_HEADS ** -0.5, F32))),
    )


def _ret_weights(w_in):
    hq, hv = RET_HEADS * RET_QK_DIM, RET_HEADS * RET_V_DIM
    parts = (w_in[:, :hq], w_in[:, hq:2 * hq], w_in[:, 2 * hq:2 * hq + hv], w_in[:, 2 * hq + hv:])
    return [a.astype(MXU_DTYPE) for a in parts]


def _ret_groups():
    half = RET_QK_DIM // 2
    hq, hv = RET_HEADS * RET_QK_DIM, RET_HEADS * RET_V_DIM
    return (
        (half, hq, ((0, hq, 1.0, F32),)),
        (half, hq, ((0, hq, RET_QK_DIM ** -0.5, F32),)),
        (0, 0, ((0, hv, 1.0, F32),)),
        (0, 0, ((0, hv, 1.0, F32),)),
    )


def kernel(x_prompt, x_sample, cache_k, cache_v, cache_idx_k, state_ret, page_table, w_in_attn, w_out_attn, w_in_ret, ret_norm_gain, w_out_ret, w_router, b_router, w_exp_gate, w_exp_up, w_exp_down, ln_gain, ln_bias):
    batch, t_len, _ = x_prompt.shape
    dec_batch, ts, _ = x_sample.shape
    n_pages = page_table.shape[1]
    past = n_pages * PAGE_SIZE
    pool = cache_k.shape[1]
    n_p, n_s = batch * t_len, dec_batch * ts
    topk_p = min(IDX_TOPK_MAX, t_len // 4)
    topk_s = min(IDX_TOPK_MAX, (past + ts) // 4)
    tm_p = _token_tile(t_len, 512)
    tm_s = _token_tile(n_s, 512)
    pos_p = jnp.arange(t_len, dtype=jnp.int32)
    pos_s = past + (jnp.arange(tm_s, dtype=jnp.int32) % ts)

    xp = x_prompt.reshape(n_p, D_MODEL)
    xs = x_sample.reshape(n_s, D_MODEL)

    wr = jnp.concatenate([w_router, jnp.zeros((D_MODEL, LANES - N_EXPERTS), F32)], axis=1)
    br = jnp.concatenate([b_router, jnp.zeros((LANES - N_EXPERTS,), F32)])[None, :]

    def moe(x2d, i, tm):
        return _moe_ln(x2d, wr, br, w_exp_gate[i].astype(MXU_DTYPE), w_exp_up[i].astype(MXU_DTYPE),
                       w_exp_down[i].astype(MXU_DTYPE), ln_gain[i, 1][None, :], ln_bias[i, 1][None, :], tm)

    a = 0
    aw = _attn_weights(w_in_attn[a])
    w_out = w_out_attn[a].astype(MXU_DTYPE)
    g0, b0 = ln_gain[0, 0][None, :], ln_bias[0, 0][None, :]

    cos_p, sin_p = _rope_tables(pos_p, ATTN_HEAD_DIM, ATTN_WIDTH)
    q_p, k_p, kb_p, v_p, vb_p, qi_p, ki_p, kk_p, wi_p = _project(
        xp, aw, _attn_groups(MXU_DTYPE), cos_p, sin_p, t_len // tm_p, tm_p)
    bias_p = _prompt_index_mask(qi_p, wi_p, kk_p, batch, t_len, topk_p)
    o_p = _prompt_attention(q_p, bias_p, kb_p, vb_p, batch, t_len)
    xp = _outproj_ln(o_p, xp, w_out, g0, b0, tm_p)

    cos_s, sin_s = _rope_tables(pos_s, ATTN_HEAD_DIM, ATTN_WIDTH)
    q_s, k_s, _, v_s, _, qi_s, ki_s, _, wi_s = _project(
        xs, aw, _attn_groups(F32), cos_s, sin_s, 1, tm_s)
    cache_ikt = jnp.transpose(cache_idx_k[a], (0, 2, 1))
    scores_s = _sample_scores(page_table, qi_s, wi_s, ki_s, cache_ikt, dec_batch, ts)
    bias_s = _sample_mask(scores_s.reshape(n_s, past + LANES), topk_s).reshape(dec_batch, ts, past + LANES)
    cache_kt = jnp.transpose(cache_k[a], (0, 2, 3, 1)).reshape(pool, ATTN_WIDTH, PAGE_SIZE)
    cache_vt = jnp.transpose(cache_v[a], (0, 2, 3, 1)).reshape(pool, ATTN_WIDTH, PAGE_SIZE)
    o_s = _sample_attention(page_table, q_s, bias_s, k_s, v_s, cache_kt, cache_vt, dec_batch, ts)
    xs = _outproj_ln(o_s, xs, w_out, g0, b0, tm_s)

    new_k_p = k_p.reshape(1, batch, t_len, ATTN_HEADS, ATTN_HEAD_DIM)
    new_v_p = v_p.reshape(1, batch, t_len, ATTN_HEADS, ATTN_HEAD_DIM)
    new_ik_p = ki_p.reshape(1, batch, t_len, IDX_DIM)
    new_k_s = k_s.reshape(1, dec_batch, ts, ATTN_HEADS, ATTN_HEAD_DIM)
    new_v_s = v_s.reshape(1, dec_batch, ts, ATTN_HEADS, ATTN_HEAD_DIM)
    new_ik_s = ki_s.reshape(1, dec_batch, ts, IDX_DIM)

    xp = moe(xp, 0, _token_tile(n_p, 1024))
    xs = moe(xs, 0, _token_tile(n_s, 1024))

    r = 0
    rw = _ret_weights(w_in_ret[r])
    w_out_r = w_out_ret[r].astype(MXU_DTYPE)
    gain = ret_norm_gain[r][None, :]
    g1, b1 = ln_gain[1, 0][None, :], ln_bias[1, 0][None, :]
    hq = RET_HEADS * RET_QK_DIM
    tm_rp = _token_tile(t_len, 256)
    tm_rs = _token_tile(n_s, 256)

    cos_p, sin_p = _rope_tables(pos_p, RET_QK_DIM, hq)
    rq, rk, rv, rg = _project(xp, rw, _ret_groups(), cos_p, sin_p, t_len // tm_rp, tm_rp)
    state0 = jnp.zeros((batch, RET_HEADS, RET_QK_DIM, RET_V_DIM), F32)
    ro, state_p = _retention_scan(rq, rk, rv, state0, batch, t_len // RET_CHUNK, RET_CHUNK)
    xp = _retention_output_ln(ro, rg, gain, xp, w_out_r, g1, b1, tm_rp)

    cos_s, sin_s = _rope_tables(past + (jnp.arange(tm_rs, dtype=jnp.int32) % ts), RET_QK_DIM, hq)
    rq, rk, rv, rg = _project(xs, rw, _ret_groups(), cos_s, sin_s, 1, tm_rs)
    ro, state_s = _retention_scan(rq, rk, rv, state_ret[r], dec_batch, 1, ts)
    xs = _retention_output_ln(ro, rg, gain, xs, w_out_r, g1, b1, tm_rs)

    xp = moe(xp, 1, _token_tile(n_p, 1024))
    xs = moe(xs, 1, _token_tile(n_s, 1024))

    return (xp.reshape(batch, t_len, D_MODEL), xs.reshape(dec_batch, ts, D_MODEL),
            new_k_p, new_v_p, new_ik_p, new_k_s, new_v_s, new_ik_s,
            state_p[None], state_s[None])
```

```python
import functools

import jax
import jax.numpy as jnp
from jax import lax
from jax.experimental import pallas as pl
from jax.experimental.pallas import tpu as pltpu

F32 = jnp.float32
BF16 = jnp.bfloat16
MXU_DTYPE = jnp.bfloat16

D_MODEL = 1024
PAGE_SIZE = 128
ATTN_HEADS = 16
ATTN_HEAD_DIM = 64
ATTN_WIDTH = ATTN_HEADS * ATTN_HEAD_DIM
IDX_HEADS = 8
IDX_DIM = 64
IDX_TOPK_MAX = 256
Q_BLOCK = 128
ROPE_THETA = 10000.0
RET_HEADS = 4
RET_QK_DIM = D_MODEL // RET_HEADS
RET_V_DIM = 2 * D_MODEL // RET_HEADS
RET_CHUNK = 128
N_EXPERTS = 16
N_GROUPS = 4
EXPERTS_PER_GROUP = N_EXPERTS // N_GROUPS
EXPERT_DIM = D_MODEL // 2
DEPTH = 2
DEEPNORM_ALPHA = (2 * DEPTH) ** 0.25
LN_EPS = 1e-5

LANES = 128
SUBLANES = 8
VMEM_LIMIT_BYTES = 58 * 1024 * 1024

NEG_INF = float("-inf")


def _cparams(*sem):
    return pltpu.CompilerParams(dimension_semantics=sem, vmem_limit_bytes=VMEM_LIMIT_BYTES)


def _mm(a, b):
    return jnp.dot(a, b, preferred_element_type=F32)


def _mm_nt(a, b):
    return lax.dot_general(a, b, (((1,), (1,)), ((), ())), preferred_element_type=F32)


def _mm_tn(a, b):
    return lax.dot_general(a, b, (((0,), (0,)), ((), ())), preferred_element_type=F32)


def _layer_norm(z, g, b):
    mu = jnp.mean(z, axis=-1, keepdims=True)
    d = z - mu
    var = jnp.mean(d * d, axis=-1, keepdims=True)
    return d * lax.rsqrt(var + LN_EPS) * g + b


def _sigmoid(x):
    return 1.0 / (1.0 + jnp.exp(-x))


def _rope_tables(pos, head_dim, width):
    half = head_dim // 2
    inv = ROPE_THETA ** (-jnp.arange(0, head_dim, 2, dtype=F32) / head_dim)
    ang = pos.astype(F32)[:, None] * inv[None, :]
    cos, sin = jnp.cos(ang), jnp.sin(ang)
    cos_h = jnp.concatenate([cos, cos], axis=1)
    sin_h = jnp.concatenate([-sin, sin], axis=1)
    reps = width // head_dim
    return jnp.tile(cos_h, (1, reps)), jnp.tile(sin_h, (1, reps))


def _rope(y, cos, sin, half):
    w = y.shape[-1]
    lane = lax.broadcasted_iota(jnp.int32, y.shape, 1)
    first = (lane % (2 * half)) < half
    swapped = jnp.where(first, pltpu.roll(y, w - half, 1), pltpu.roll(y, half, 1))
    return y * cos + swapped * sin


def _proj_kernel(x_ref, cos_ref, sin_ref, *refs, groups):
    n_g = len(groups)
    w_refs, out_refs = refs[:n_g], refs[n_g:]
    xb = x_ref[...].astype(MXU_DTYPE)
    oi = 0
    for g, (rope_half, rope_w, outs) in enumerate(groups):
        y = _mm(xb, w_refs[g][...])
        if rope_half:
            yr = _rope(y[:, :rope_w], cos_ref[:, :rope_w], sin_ref[:, :rope_w], rope_half)
            y = yr if rope_w == y.shape[1] else jnp.concatenate([yr, y[:, rope_w:]], axis=1)
        for (c0, w, scale, dt) in outs:
            piece = y[:, c0:c0 + w]
            if scale != 1.0:
                piece = piece * scale
            out_refs[oi][...] = piece.astype(dt)
            oi += 1


def _token_tile(n, cap):
    tm = min(cap, n)
    assert n % tm == 0 and tm % SUBLANES == 0
    return tm


def _project(x2d, weights, groups, cos, sin, n_tab_blocks, tm):
    n = x2d.shape[0]
    out_shapes, out_specs = [], []
    for (_, _, outs) in groups:
        for (_, w, _, dt) in outs:
            out_shapes.append(jax.ShapeDtypeStruct((n, w), dt))
            out_specs.append(pl.BlockSpec((tm, w), lambda m: (m, 0)))
    tw = cos.shape[1]
    in_specs = [pl.BlockSpec((tm, D_MODEL), lambda m: (m, 0)),
                pl.BlockSpec((tm, tw), lambda m: (m % n_tab_blocks, 0)),
                pl.BlockSpec((tm, tw), lambda m: (m % n_tab_blocks, 0))]
    for w in weights:
        in_specs.append(pl.BlockSpec(w.shape, lambda m: (0, 0)))
    return pl.pallas_call(
        functools.partial(_proj_kernel, groups=groups),
        grid=(n // tm,),
        in_specs=in_specs,
        out_specs=out_specs,
        out_shape=out_shapes,
        compiler_params=_cparams("parallel"),
        name="proj",
    )(x2d, cos, sin, *weights)


def _row_count(pred):
    return jnp.sum(jnp.where(pred, 1.0, 0.0), axis=1, keepdims=True)


KEY_NEG_INF = -2139095041


def _key_to_float(key):
    bits = key ^ ((key >> 31) & jnp.int32(0x7FFFFFFF))
    return jnp.where(key <= KEY_NEG_INF, NEG_INF, lax.bitcast_convert_type(bits, F32))


def _topk_select(score, k):
    r, s = score.shape
    kf = jnp.float32(k)

    def value_bit(i, carry):
        t, cnt = carry
        cand = t + lax.shift_left(jnp.int32(1), jnp.int32(31) - i)
        c = _row_count(score >= _key_to_float(cand))
        ok = c >= kf
        return jnp.where(ok, cand, t), jnp.where(ok, c, cnt)

    t, c_ge = lax.fori_loop(0, 32, value_bit,
                            (jnp.full((r, 1), jnp.iinfo(jnp.int32).min, jnp.int32),
                             jnp.full((r, 1), float(s), F32)))
    thr = _key_to_float(t)
    gt = score > thr
    eq = score == thr
    need = kf - _row_count(gt)
    col = lax.broadcasted_iota(jnp.int32, (r, s), 1)
    n_bits = max(1, (s - 1).bit_length())

    def last_tie_column():
        def index_bit(i, j):
            cand = j + lax.shift_left(jnp.int32(1), jnp.int32(n_bits - 1) - i)
            return jnp.where(_row_count(eq & (col < cand)) < need, cand, j)
        return lax.fori_loop(0, n_bits, index_bit, jnp.zeros((r, 1), jnp.int32))

    j = lax.cond(jnp.max(c_ge) > kf, last_tie_column, lambda: jnp.full((r, 1), s, jnp.int32))
    keep = gt | (eq & (col <= j))
    return keep & (score > NEG_INF)


def _causal_widths(t_len):
    step = min(2 * Q_BLOCK, t_len)
    assert t_len % step == 0
    return tuple(range(step, t_len + 1, step))


def _for_causal_width(n_keys, widths, run, enable=True):
    prev = 0
    for w in widths:
        pl.when(enable & (n_keys > prev) & (n_keys <= w))(functools.partial(run, w))
        prev = w


def _idx_mask_kernel(qi_ref, wi_ref, kk_ref, bias_ref, *, topk, widths):
    i = pl.program_id(1)
    qb, t_len = qi_ref.shape[0], kk_ref.shape[0]
    n_keys = (i + 1) * qb
    few = n_keys <= topk

    @pl.when(few)
    def _():
        row = i * qb + lax.broadcasted_iota(jnp.int32, (qb, t_len), 0)
        col = lax.broadcasted_iota(jnp.int32, (qb, t_len), 1)
        bias_ref[...] = jnp.where(col <= row, 0.0, NEG_INF).astype(bias_ref.dtype)

    def run(w):
        qi = qi_ref[...]
        kk = kk_ref[:w, :]
        wi = wi_ref[...]
        lane = lax.broadcasted_iota(jnp.int32, (qb, LANES), 1)
        zero = jnp.zeros((qb, LANES), qi.dtype)
        score = jnp.zeros((qb, w), F32)
        for p in range(IDX_HEADS // 2):
            qp = qi[:, LANES * p:LANES * (p + 1)]
            for half in range(2):
                lhs = jnp.where(lane < IDX_DIM, qp, zero) if half == 0 else jnp.where(lane >= IDX_DIM, qp, zero)
                h = 2 * p + half
                score = score + jnp.maximum(_mm_nt(lhs, kk), 0.0) * wi[:, h:h + 1]
        row = i * qb + lax.broadcasted_iota(jnp.int32, (qb, w), 0)
        col = lax.broadcasted_iota(jnp.int32, (qb, w), 1)
        score = jnp.where(col <= row, score, NEG_INF)
        keep = _topk_select(score, topk)
        bias_ref[:, :w] = jnp.where(keep, 0.0, NEG_INF).astype(bias_ref.dtype)
        if w < t_len:
            bias_ref[:, w:] = jnp.full((qb, t_len - w), NEG_INF, bias_ref.dtype)

    _for_causal_width(n_keys, widths, run, enable=jnp.logical_not(few))


def _prompt_index_mask(qi, wi, kk, batch, t_len, topk):
    nb = t_len // Q_BLOCK
    return pl.pallas_call(
        functools.partial(_idx_mask_kernel, topk=topk, widths=_causal_widths(t_len)),
        grid=(batch, nb),
        in_specs=[pl.BlockSpec((Q_BLOCK, qi.shape[1]), lambda b, i: (b * nb + i, 0)),
                  pl.BlockSpec((Q_BLOCK, LANES), lambda b, i: (b * nb + i, 0)),
                  pl.BlockSpec((t_len, LANES), lambda b, i: (b, 0))],
        out_specs=pl.BlockSpec((Q_BLOCK, t_len), lambda b, i: (b * nb + i, 0)),
        out_shape=jax.ShapeDtypeStruct((batch * t_len, t_len), BF16),
        compiler_params=_cparams("parallel", "arbitrary"),
        name="prompt_index_mask",
    )(qi, wi, kk)


def _attn_kernel(q_ref, bias_ref, k_ref, v_ref, o_ref, *, widths):
    i = pl.program_id(1)
    qb = q_ref.shape[0]

    def run(w):
        bias = bias_ref[:, :w].astype(F32)
        bias2 = jnp.concatenate([bias, bias], axis=0)
        lane = lax.broadcasted_iota(jnp.int32, (qb, LANES), 1)
        low = lane < ATTN_HEAD_DIM
        zero = jnp.zeros((qb, LANES), q_ref.dtype)
        for p in range(ATTN_HEADS // 2):
            sl = slice(LANES * p, LANES * (p + 1))
            qp = q_ref[:, sl]
            lhs = jnp.concatenate([jnp.where(low, qp, zero), jnp.where(low, zero, qp)], axis=0)
            s = _mm_nt(lhs, k_ref[:w, sl]) + bias2
            m = jnp.max(s, axis=1, keepdims=True)
            e = jnp.exp(s - m)
            l = jnp.sum(e, axis=1, keepdims=True)
            pv = _mm(e.astype(MXU_DTYPE), v_ref[:w, sl]) / l
            o_ref[:, sl] = jnp.where(low, pv[:qb], pv[qb:]).astype(o_ref.dtype)

    _for_causal_width((i + 1) * qb, widths, run)


def _prompt_attention(q, bias, k, v, batch, t_len):
    nb = t_len // Q_BLOCK
    return pl.pallas_call(
        functools.partial(_attn_kernel, widths=_causal_widths(t_len)),
        grid=(batch, nb),
        in_specs=[pl.BlockSpec((Q_BLOCK, ATTN_WIDTH), lambda b, i: (b * nb + i, 0)),
                  pl.BlockSpec((Q_BLOCK, t_len), lambda b, i: (b * nb + i, 0)),
                  pl.BlockSpec((t_len, ATTN_WIDTH), lambda b, i: (b, 0)),
                  pl.BlockSpec((t_len, ATTN_WIDTH), lambda b, i: (b, 0))],
        out_specs=pl.BlockSpec((Q_BLOCK, ATTN_WIDTH), lambda b, i: (b * nb + i, 0)),
        out_shape=jax.ShapeDtypeStruct((batch * t_len, ATTN_WIDTH), MXU_DTYPE),
        compiler_params=_cparams("parallel", "arbitrary"),
        name="prompt_attention",
    )(q, bias, k, v)


def _sample_score_kernel(pt_ref, qi_ref, wi_ref, kin_ref, *refs, n_pages):
    del pt_ref
    page_refs, out_ref = refs[:n_pages], refs[n_pages]
    ts = qi_ref.shape[0]
    qi = qi_ref[...]
    wi = wi_ref[...]
    lhs = jnp.concatenate([qi[:, IDX_DIM * h:IDX_DIM * (h + 1)] for h in range(IDX_HEADS)],
                          axis=0).astype(MXU_DTYPE)
    wcol = jnp.concatenate([wi[:, h:h + 1] for h in range(IDX_HEADS)], axis=0)

    def head_sum(s):
        w = jnp.maximum(s, 0.0) * wcol
        acc = w[0:ts]
        for h in range(1, IDX_HEADS):
            acc = acc + w[ts * h:ts * (h + 1)]
        return acc

    for j in range(n_pages):
        kpt = page_refs[j][0].astype(MXU_DTYPE)
        out_ref[0, :, PAGE_SIZE * j:PAGE_SIZE * (j + 1)] = head_sum(_mm(lhs, kpt))
    kn = kin_ref[...].astype(MXU_DTYPE)
    kn = jnp.concatenate([kn, jnp.zeros((LANES - ts, IDX_DIM), MXU_DTYPE)], axis=0)
    sn = head_sum(_mm_nt(lhs, kn))
    row = lax.broadcasted_iota(jnp.int32, sn.shape, 0)
    col = lax.broadcasted_iota(jnp.int32, sn.shape, 1)
    past = PAGE_SIZE * n_pages
    out_ref[0, :, past:past + LANES] = jnp.where(col <= row, sn, NEG_INF)


def _sample_scores(page_table, qi, wi, ki_new, cache_ikt, dec_batch, ts):
    n_pages = page_table.shape[1]
    past = n_pages * PAGE_SIZE

    def page_spec(j):
        return pl.BlockSpec((1, IDX_DIM, PAGE_SIZE), lambda b, pt: (pt[b, j], 0, 0))

    grid_spec = pltpu.PrefetchScalarGridSpec(
        num_scalar_prefetch=1,
        grid=(dec_batch,),
        in_specs=[pl.BlockSpec((ts, qi.shape[1]), lambda b, pt: (b, 0)),
                  pl.BlockSpec((ts, LANES), lambda b, pt: (b, 0)),
                  pl.BlockSpec((ts, IDX_DIM), lambda b, pt: (b, 0))]
                 + [page_spec(j) for j in range(n_pages)],
        out_specs=pl.BlockSpec((1, ts, past + LANES), lambda b, pt: (b, 0, 0)),
    )
    return pl.pallas_call(
        functools.partial(_sample_score_kernel, n_pages=n_pages),
        grid_spec=grid_spec,
        out_shape=jax.ShapeDtypeStruct((dec_batch, ts, past + LANES), F32),
        compiler_params=_cparams("arbitrary"),
        name="sample_scores",
    )(page_table, qi, wi, ki_new, *([cache_ikt] * n_pages))


def _mask_kernel(score_ref, bias_ref, *, topk):
    keep = _topk_select(score_ref[...], topk)
    bias_ref[...] = jnp.where(keep, 0.0, NEG_INF).astype(bias_ref.dtype)


def _sample_mask(scores2d, topk):
    n, s = scores2d.shape
    rb = _token_tile(n, Q_BLOCK)
    return pl.pallas_call(
        functools.partial(_mask_kernel, topk=topk),
        grid=(n // rb,),
        in_specs=[pl.BlockSpec((rb, s), lambda i: (i, 0))],
        out_specs=pl.BlockSpec((rb, s), lambda i: (i, 0)),
        out_shape=jax.ShapeDtypeStruct((n, s), BF16),
        compiler_params=_cparams("parallel"),
        name="sample_mask",
    )(scores2d)


def _head_lane_mask(shape, h):
    lane = lax.broadcasted_iota(jnp.int32, shape, 1)
    return (lane >= ATTN_HEAD_DIM * h) & (lane < ATTN_HEAD_DIM * (h + 1))


def _sample_attn_kernel(pt_ref, q_ref, bias_ref, kn_ref, vn_ref, *refs, n_pages):
    del pt_ref
    k_pages, v_pages, o_ref = refs[:n_pages], refs[n_pages:2 * n_pages], refs[2 * n_pages]
    ts = q_ref.shape[0]
    past = n_pages * PAGE_SIZE
    q = q_ref[...]
    qbd = jnp.concatenate([jnp.where(_head_lane_mask(q.shape, h), q, 0.0)
                           for h in range(ATTN_HEADS)], axis=0).astype(MXU_DTYPE)
    pad = jnp.zeros((LANES - ts, ATTN_WIDTH), F32)
    kn = jnp.concatenate([kn_ref[...], pad], axis=0).astype(MXU_DTYPE)
    vn = jnp.concatenate([vn_ref[...], pad], axis=0).astype(MXU_DTYPE)
    s = jnp.concatenate([_mm(qbd, k_pages[pg][0].astype(MXU_DTYPE)) for pg in range(n_pages)]
                        + [_mm_nt(qbd, kn)], axis=1)
    s = s + jnp.concatenate([bias_ref[0].astype(F32)] * ATTN_HEADS, axis=0)
    m = jnp.max(s, axis=1, keepdims=True)
    e = jnp.exp(s - m)
    l = jnp.sum(e, axis=1, keepdims=True)
    eb = e.astype(MXU_DTYPE)
    pv = _mm(eb[:, past:], vn)
    for pg in range(n_pages):
        pv = pv + _mm_nt(eb[:, PAGE_SIZE * pg:PAGE_SIZE * (pg + 1)], v_pages[pg][0].astype(MXU_DTYPE))
    out = pv / l
    o = jnp.zeros((ts, ATTN_WIDTH), F32)
    for h in range(ATTN_HEADS):
        o = o + jnp.where(_head_lane_mask(o.shape, h), out[ts * h:ts * (h + 1)], 0.0)
    o_ref[...] = o.astype(o_ref.dtype)


def _sample_attention(page_table, q, bias, k_new, v_new, cache_kt, cache_vt, dec_batch, ts):
    n_pages = page_table.shape[1]
    past = n_pages * PAGE_SIZE

    def page_spec(g):
        return pl.BlockSpec((1, ATTN_WIDTH, PAGE_SIZE), lambda b, pt: (pt[b, g], 0, 0))

    grid_spec = pltpu.PrefetchScalarGridSpec(
        num_scalar_prefetch=1,
        grid=(dec_batch,),
        in_specs=[pl.BlockSpec((ts, ATTN_WIDTH), lambda b, pt: (b, 0)),
                  pl.BlockSpec((1, ts, past + LANES), lambda b, pt: (b, 0, 0)),
                  pl.BlockSpec((ts, ATTN_WIDTH), lambda b, pt: (b, 0)),
                  pl.BlockSpec((ts, ATTN_WIDTH), lambda b, pt: (b, 0))]
                 + [page_spec(g) for g in range(n_pages)] * 2,
        out_specs=pl.BlockSpec((ts, ATTN_WIDTH), lambda b, pt: (b, 0)),
    )
    return pl.pallas_call(
        functools.partial(_sample_attn_kernel, n_pages=n_pages),
        grid_spec=grid_spec,
        out_shape=jax.ShapeDtypeStruct((dec_batch * ts, ATTN_WIDTH), MXU_DTYPE),
        compiler_params=_cparams("parallel"),
        name="sample_attention",
    )(page_table, q, bias, k_new, v_new, *([cache_kt] * n_pages), *([cache_vt] * n_pages))


def _outproj_ln_kernel(o_ref, x_ref, w_ref, g_ref, b_ref, y_ref):
    m = _mm(o_ref[...].astype(MXU_DTYPE), w_ref[...])
    y_ref[...] = _layer_norm(DEEPNORM_ALPHA * x_ref[...] + m, g_ref[...], b_ref[...])


def _outproj_ln(o, x2d, w, g, b, tm):
    n = x2d.shape[0]
    return pl.pallas_call(
        _outproj_ln_kernel,
        grid=(n // tm,),
        in_specs=[pl.BlockSpec((tm, o.shape[1]), lambda m: (m, 0)),
                  pl.BlockSpec((tm, D_MODEL), lambda m: (m, 0)),
                  pl.BlockSpec(w.shape, lambda m: (0, 0)),
                  pl.BlockSpec((1, D_MODEL), lambda m: (0, 0)),
                  pl.BlockSpec((1, D_MODEL), lambda m: (0, 0))],
        out_specs=pl.BlockSpec((tm, D_MODEL), lambda m: (m, 0)),
        out_shape=jax.ShapeDtypeStruct((n, D_MODEL), F32),
        compiler_params=_cparams("parallel"),
        name="outproj_ln",
    )(o, x2d, w, g, b)


def _retention_tables(chunk):
    lg = jnp.log(1.0 - 2.0 ** (-5.0 - jnp.arange(RET_HEADS, dtype=F32)))
    i = jnp.arange(chunk, dtype=F32)
    diff = i[:, None] - i[None, :]
    dmask = jnp.where(diff[None] >= 0, jnp.exp(jnp.maximum(diff, 0.0)[None] * lg[:, None, None]), 0.0)
    cross = jnp.exp((i + 1.0)[:, None] * lg[None, :])
    kdec = jnp.exp((chunk - 1.0 - i)[:, None] * lg[None, :])
    cdec = jnp.exp(chunk * lg)
    pad = jnp.zeros((chunk, LANES - RET_HEADS), F32)
    cross = jnp.concatenate([cross, pad], axis=1)
    kdec = jnp.concatenate([kdec, pad], axis=1)
    cdec = jnp.broadcast_to(cdec[:, None, None], (RET_HEADS, 1, RET_V_DIM))
    return dmask.astype(F32), cross, kdec, cdec


def _retention_kernel(q_ref, k_ref, v_ref, s0_ref, dmask_ref, cross_ref, kdec_ref, cdec_ref,
                      o_ref, s_out_ref, state_sc):
    c = pl.program_id(1)

    @pl.when(c == 0)
    def _():
        state_sc[...] = s0_ref[0]

    cross = cross_ref[...]
    kdec = kdec_ref[...]
    for h in range(RET_HEADS):
        qs = slice(RET_QK_DIM * h, RET_QK_DIM * (h + 1))
        vs = slice(RET_V_DIM * h, RET_V_DIM * (h + 1))
        qh = q_ref[:, qs]
        kh = k_ref[:, qs]
        qb = qh.astype(MXU_DTYPE)
        vb = v_ref[:, vs].astype(MXU_DTYPE)
        state = state_sc[h]
        inner = _mm_nt(qb, kh.astype(MXU_DTYPE)) * dmask_ref[h]
        o = _mm(inner.astype(MXU_DTYPE), vb) + _mm(qb, state.astype(MXU_DTYPE)) * cross[:, h:h + 1]
        o_ref[:, vs] = o
        kd = (kh * kdec[:, h:h + 1]).astype(MXU_DTYPE)
        state_sc[h] = cdec_ref[h] * state + _mm_tn(kd, vb)

    @pl.when(c == pl.num_programs(1) - 1)
    def _():
        s_out_ref[0] = state_sc[...]


def _retention_scan(q, k, v, state0, batch, n_chunks, chunk):
    dmask, cross, kdec, cdec = _retention_tables(chunk)
    hq, hv = RET_HEADS * RET_QK_DIM, RET_HEADS * RET_V_DIM
    state_block = (1, RET_HEADS, RET_QK_DIM, RET_V_DIM)
    return pl.pallas_call(
        _retention_kernel,
        grid=(batch, n_chunks),
        in_specs=[pl.BlockSpec((chunk, hq), lambda b, c: (b * n_chunks + c, 0)),
                  pl.BlockSpec((chunk, hq), lambda b, c: (b * n_chunks + c, 0)),
                  pl.BlockSpec((chunk, hv), lambda b, c: (b * n_chunks + c, 0)),
                  pl.BlockSpec(state_block, lambda b, c: (b, 0, 0, 0)),
                  pl.BlockSpec(dmask.shape, lambda b, c: (0, 0, 0)),
                  pl.BlockSpec(cross.shape, lambda b, c: (0, 0)),
                  pl.BlockSpec(kdec.shape, lambda b, c: (0, 0)),
                  pl.BlockSpec(cdec.shape, lambda b, c: (0, 0, 0))],
        out_specs=[pl.BlockSpec((chunk, hv), lambda b, c: (b * n_chunks + c, 0)),
                   pl.BlockSpec(state_block, lambda b, c: (b, 0, 0, 0))],
        out_shape=[jax.ShapeDtypeStruct((batch * n_chunks * chunk, hv), F32),
                   jax.ShapeDtypeStruct((batch,) + state_block[1:], F32)],
        scratch_shapes=[pltpu.VMEM(state_block[1:], F32)],
        compiler_params=_cparams("parallel", "arbitrary"),
        name="retention_scan",
    )(q, k, v, state0, dmask, cross, kdec, cdec)


def _ret_out_kernel(o_ref, gate_ref, gain_ref, x_ref, w_ref, g_ref, b_ref, y_ref):
    pieces = []
    for h in range(RET_HEADS):
        vs = slice(RET_V_DIM * h, RET_V_DIM * (h + 1))
        of = o_ref[:, vs]
        mu = jnp.mean(of, axis=-1, keepdims=True)
        d = of - mu
        var = jnp.mean(d * d, axis=-1, keepdims=True)
        on = d * lax.rsqrt(var + LN_EPS) * gain_ref[:, vs]
        gt = gate_ref[:, vs]
        pieces.append(((gt * _sigmoid(gt)) * on).astype(MXU_DTYPE))
    m = _mm(jnp.concatenate(pieces, axis=1), w_ref[...])
    y_ref[...] = _layer_norm(DEEPNORM_ALPHA * x_ref[...] + m, g_ref[...], b_ref[...])


def _retention_output_ln(o, gate, gain, x2d, w, g, b, tm):
    n = x2d.shape[0]
    hv = RET_HEADS * RET_V_DIM
    return pl.pallas_call(
        _ret_out_kernel,
        grid=(n // tm,),
        in_specs=[pl.BlockSpec((tm, hv), lambda m: (m, 0)),
                  pl.BlockSpec((tm, hv), lambda m: (m, 0)),
                  pl.BlockSpec((1, hv), lambda m: (0, 0)),
                  pl.BlockSpec((tm, D_MODEL), lambda m: (m, 0)),
                  pl.BlockSpec(w.shape, lambda m: (0, 0)),
                  pl.BlockSpec((1, D_MODEL), lambda m: (0, 0)),
                  pl.BlockSpec((1, D_MODEL), lambda m: (0, 0))],
        out_specs=pl.BlockSpec((tm, D_MODEL), lambda m: (m, 0)),
        out_shape=jax.ShapeDtypeStruct((n, D_MODEL), F32),
        compiler_params=_cparams("parallel"),
        name="retention_out_ln",
    )(o, gate, gain, x2d, w, g, b)


def _route(logits, b_router):
    shape = logits.shape
    lane = lax.broadcasted_iota(jnp.int32, shape, 1)
    lane_f = lane.astype(F32)
    valid = lane < N_EXPERTS
    lg = jnp.where(valid, logits, NEG_INF)
    ex = jnp.exp(lg - jnp.max(lg, axis=1, keepdims=True))
    probs = ex / jnp.sum(ex, axis=1, keepdims=True)
    sel = probs + b_router
    far = jnp.float32(LANES)

    def first_max(v):
        m = jnp.max(v, axis=1, keepdims=True)
        idx = jnp.min(jnp.where(v == m, lane_f, far), axis=1, keepdims=True)
        return m, idx

    gscore = []
    for g in range(N_GROUPS):
        in_g = (lane >= EXPERTS_PER_GROUP * g) & (lane < EXPERTS_PER_GROUP * (g + 1))
        sg = jnp.where(in_g, sel, NEG_INF)
        m1, i1 = first_max(sg)
        m2 = jnp.max(jnp.where(lane_f == i1, NEG_INF, sg), axis=1, keepdims=True)
        gscore.append(m1 + m2)
    best, gbest = gscore[0], jnp.zeros_like(gscore[0])
    for g in range(1, N_GROUPS):
        better = gscore[g] > best
        best = jnp.where(better, gscore[g], best)
        gbest = jnp.where(better, jnp.float32(g), gbest)
    lo = gbest * EXPERTS_PER_GROUP
    in_best = (lane_f >= lo) & (lane_f < lo + EXPERTS_PER_GROUP)
    sm = jnp.where(in_best, sel, NEG_INF)
    _, e1 = first_max(sm)
    _, e2 = first_max(jnp.where(lane_f == e1, NEG_INF, sm))
    is1, is2 = lane_f == e1, lane_f == e2
    p1 = jnp.sum(jnp.where(is1, probs, 0.0), axis=1, keepdims=True)
    p2 = jnp.sum(jnp.where(is2, probs, 0.0), axis=1, keepdims=True)
    den = p1 + p2
    return jnp.where(is1, p1 / den, 0.0) + jnp.where(is2, p2 / den, 0.0), gbest


def _router_logits(x, wr):
    return jnp.dot(x, wr, preferred_element_type=F32, precision=lax.Precision.HIGHEST)


def _expert_ffn(xb, gate, first_expert, wg_ref, wu_ref, wd_ref, n_experts):
    lane = lax.broadcasted_iota(jnp.int32, gate.shape, 1)
    acc = None
    for e in range(n_experts):
        ge = jnp.sum(jnp.where(lane == first_expert + e, gate, 0.0), axis=1, keepdims=True)
        hg = _mm(xb, wg_ref[e])
        hu = _mm(xb, wu_ref[e])
        h = (hg * _sigmoid(hg)) * hu * ge
        y = _mm(h.astype(MXU_DTYPE), wd_ref[e])
        acc = y if acc is None else acc + y
    return acc


def _moe_kernel(x_ref, wr_ref, br_ref, wg_ref, wu_ref, wd_ref, g_ref, b_ref, y_ref,
                xb_sc, gate_sc, acc_sc):
    e = pl.program_id(1)

    @pl.when(e == 0)
    def _():
        x = x_ref[...]
        xb_sc[...] = x.astype(MXU_DTYPE)
        gate_sc[...] = _route(_router_logits(x, wr_ref[...]), br_ref[...])[0]
        acc_sc[...] = jnp.zeros(acc_sc.shape, F32)

    acc_sc[...] += _expert_ffn(xb_sc[...], gate_sc[...], e, wg_ref, wu_ref, wd_ref, 1)

    @pl.when(e == pl.num_programs(1) - 1)
    def _():
        y_ref[...] = _layer_norm(DEEPNORM_ALPHA * x_ref[...] + acc_sc[...], g_ref[...], b_ref[...])


def _moe_ln(x2d, wr, br, wg, wu, wd, g, b, tm):
    n = x2d.shape[0]
    return pl.pallas_call(
        _moe_kernel,
        grid=(n // tm, N_EXPERTS),
        in_specs=[pl.BlockSpec((tm, D_MODEL), lambda m, e: (m, 0)),
                  pl.BlockSpec(wr.shape, lambda m, e: (0, 0)),
                  pl.BlockSpec(br.shape, lambda m, e: (0, 0)),
                  pl.BlockSpec((1, D_MODEL, EXPERT_DIM), lambda m, e: (e, 0, 0)),
                  pl.BlockSpec((1, D_MODEL, EXPERT_DIM), lambda m, e: (e, 0, 0)),
                  pl.BlockSpec((1, EXPERT_DIM, D_MODEL), lambda m, e: (e, 0, 0)),
                  pl.BlockSpec((1, D_MODEL), lambda m, e: (0, 0)),
                  pl.BlockSpec((1, D_MODEL), lambda m, e: (0, 0))],
        out_specs=pl.BlockSpec((tm, D_MODEL), lambda m, e: (m, 0)),
        out_shape=jax.ShapeDtypeStruct((n, D_MODEL), F32),
        scratch_shapes=[pltpu.VMEM((tm, D_MODEL), MXU_DTYPE),
                        pltpu.VMEM((tm, LANES), F32),
                        pltpu.VMEM((tm, D_MODEL), F32)],
        compiler_params=_cparams("parallel", "arbitrary"),
        name="moe_ln",
    )(x2d, wr, br, wg, wu, wd, g, b)


SORTED_ROW_WIDTH = D_MODEL + LANES
DMA_ISSUE_UNROLL = 8


def _moe_route_kernel(x_ref, wr_ref, br_ref, gate_ref, gid_ref, rank_ref, cnt_ref, cnt_sc):
    i = pl.program_id(0)
    tm = x_ref.shape[0]

    @pl.when(i == 0)
    def _():
        cnt_sc[...] = jnp.zeros(cnt_sc.shape, F32)

    gate, gbest = _route(_router_logits(x_ref[...], wr_ref[...]), br_ref[...])
    gate_ref[...] = gate
    g_row = jnp.transpose(jnp.broadcast_to(gbest, (tm, LANES)))[0:SUBLANES, :]
    member = g_row == lax.broadcasted_iota(jnp.int32, (SUBLANES, tm), 0).astype(F32)
    onehot = jnp.where(member, 1.0, 0.0)
    r = lax.broadcasted_iota(jnp.int32, (tm, tm), 0)
    c = lax.broadcasted_iota(jnp.int32, (tm, tm), 1)
    upper = jnp.where(r < c, 1.0, 0.0).astype(MXU_DTYPE)
    before = _mm(onehot.astype(MXU_DTYPE), upper) + cnt_sc[:, 0:1]
    rank_ref[0] = jnp.sum(jnp.where(member, before, 0.0), axis=0, keepdims=True).astype(jnp.int32)
    gid_ref[0] = g_row[0:1].astype(jnp.int32)
    cnt_sc[...] = cnt_sc[...] + jnp.sum(onehot, axis=1, keepdims=True)

    @pl.when(i == pl.num_programs(0) - 1)
    def _():
        cnt_ref[...] = cnt_sc[...]


def _moe_route(x2d, wr, br, tm):
    n = x2d.shape[0]
    nt = n // tm
    return pl.pallas_call(
        _moe_route_kernel,
        grid=(nt,),
        in_specs=[pl.BlockSpec((tm, D_MODEL), lambda i: (i, 0)),
                  pl.BlockSpec(wr.shape, lambda i: (0, 0)),
                  pl.BlockSpec(br.shape, lambda i: (0, 0))],
        out_specs=[pl.BlockSpec((tm, LANES), lambda i: (i, 0)),
                   pl.BlockSpec((1, 1, tm), lambda i: (i, 0, 0)),
                   pl.BlockSpec((1, 1, tm), lambda i: (i, 0, 0)),
                   pl.BlockSpec((SUBLANES, LANES), lambda i: (0, 0))],
        out_shape=[jax.ShapeDtypeStruct((n, LANES), F32),
                   jax.ShapeDtypeStruct((nt, 1, tm), jnp.int32),
                   jax.ShapeDtypeStruct((nt, 1, tm), jnp.int32),
                   jax.ShapeDtypeStruct((SUBLANES, LANES), F32)],
        scratch_shapes=[pltpu.VMEM((SUBLANES, LANES), F32)],
        compiler_params=_cparams("arbitrary"),
        name="moe_route",
    )(x2d, wr, br)


def _copy_rows(n_rows, row_copy):
    def issue(n, carry):
        row_copy(n).start()
        return carry

    def wait(n, carry):
        row_copy(n).wait()
        return carry

    lax.fori_loop(0, n_rows, issue, 0, unroll=DMA_ISSUE_UNROLL)
    lax.fori_loop(0, n_rows, wait, 0, unroll=DMA_ISSUE_UNROLL)


def _moe_scatter_kernel(dest_ref, x_ref, gate_ref, init_ref, out_ref, row_sc, sem):
    del init_ref
    i = pl.program_id(0)
    tm = x_ref.shape[0]
    row_sc[:, :D_MODEL] = x_ref[...]
    row_sc[:, D_MODEL:] = gate_ref[...]
    _copy_rows(tm, lambda n: pltpu.make_async_copy(
        row_sc.at[pl.ds(n, 1)], out_ref.at[pl.ds(dest_ref[i * tm + n], 1)], sem))


def _moe_scatter(dest, x2d, gate, n_rows, tm):
    n = x2d.shape[0]
    init = jnp.zeros((n_rows, SORTED_ROW_WIDTH), F32)
    grid_spec = pltpu.PrefetchScalarGridSpec(
        num_scalar_prefetch=1,
        grid=(n // tm,),
        in_specs=[pl.BlockSpec((tm, D_MODEL), lambda i, d: (i, 0)),
                  pl.BlockSpec((tm, LANES), lambda i, d: (i, 0)),
                  pl.BlockSpec(memory_space=pl.ANY)],
        out_specs=pl.BlockSpec(memory_space=pl.ANY),
        scratch_shapes=[pltpu.VMEM((tm, SORTED_ROW_WIDTH), F32), pltpu.SemaphoreType.DMA(())],
    )
    return pl.pallas_call(
        _moe_scatter_kernel,
        grid_spec=grid_spec,
        out_shape=jax.ShapeDtypeStruct((n_rows, SORTED_ROW_WIDTH), F32),
        input_output_aliases={3: 0},
        compiler_params=_cparams("arbitrary"),
        name="moe_scatter",
    )(dest, x2d, gate, init)


def _moe_expert_kernel(grp_ref, in_blk_ref, used_ref, rows_ref, wg_ref, wu_ref, wd_ref, y_ref):
    del in_blk_ref
    j = pl.program_id(0)

    @pl.when(used_ref[j] > 0)
    def _():
        xb = rows_ref[:, :D_MODEL].astype(MXU_DTYPE)
        y_ref[...] = _expert_ffn(xb, rows_ref[:, D_MODEL:], EXPERTS_PER_GROUP * grp_ref[j],
                                 wg_ref, wu_ref, wd_ref, EXPERTS_PER_GROUP)

    @pl.when(used_ref[j] == 0)
    def _():
        y_ref[...] = jnp.zeros(y_ref.shape, F32)


def _moe_experts(grp, in_blk, used, rows, wg, wu, wd, tmb):
    n_tiles = grp.shape[0]
    epg = EXPERTS_PER_GROUP
    grid_spec = pltpu.PrefetchScalarGridSpec(
        num_scalar_prefetch=3,
        grid=(n_tiles,),
        in_specs=[pl.BlockSpec((tmb, SORTED_ROW_WIDTH), lambda j, g, bi, u: (bi[j], 0)),
                  pl.BlockSpec((epg, D_MODEL, EXPERT_DIM), lambda j, g, bi, u: (g[j], 0, 0)),
                  pl.BlockSpec((epg, D_MODEL, EXPERT_DIM), lambda j, g, bi, u: (g[j], 0, 0)),
                  pl.BlockSpec((epg, EXPERT_DIM, D_MODEL), lambda j, g, bi, u: (g[j], 0, 0))],
        out_specs=pl.BlockSpec((tmb, D_MODEL), lambda j, g, bi, u: (j, 0)),
    )
    return pl.pallas_call(
        _moe_expert_kernel,
        grid_spec=grid_spec,
        out_shape=jax.ShapeDtypeStruct((n_tiles * tmb, D_MODEL), F32),
        compiler_params=_cparams("arbitrary"),
        name="moe_experts",
    )(grp, in_blk, used, rows, wg, wu, wd)


def _moe_gather_ln_kernel(dest_ref, x_ref, ys_ref, g_ref, b_ref, out_ref, y_sc, sem):
    i = pl.program_id(0)
    tm = x_ref.shape[0]
    _copy_rows(tm, lambda n: pltpu.make_async_copy(
        ys_ref.at[pl.ds(dest_ref[i * tm + n], 1)], y_sc.at[pl.ds(n, 1)], sem))
    out_ref[...] = _layer_norm(DEEPNORM_ALPHA * x_ref[...] + y_sc[...], g_ref[...], b_ref[...])


def _moe_gather_ln(dest, x2d, ys, g, b, tm):
    n = x2d.shape[0]
    grid_spec = pltpu.PrefetchScalarGridSpec(
        num_scalar_prefetch=1,
        grid=(n // tm,),
        in_specs=[pl.BlockSpec((tm, D_MODEL), lambda i, d: (i, 0)),
                  pl.BlockSpec(memory_space=pl.ANY),
                  pl.BlockSpec((1, D_MODEL), lambda i, d: (0, 0)),
                  pl.BlockSpec((1, D_MODEL), lambda i, d: (0, 0))],
        out_specs=pl.BlockSpec((tm, D_MODEL), lambda i, d: (i, 0)),
        scratch_shapes=[pltpu.VMEM((tm, D_MODEL), F32), pltpu.SemaphoreType.DMA(())],
    )
    return pl.pallas_call(
        _moe_gather_ln_kernel,
        grid_spec=grid_spec,
        out_shape=jax.ShapeDtypeStruct((n, D_MODEL), F32),
        compiler_params=_cparams("arbitrary"),
        name="moe_gather_ln",
    )(dest, x2d, ys, g, b)


def _moe_sorted_ln(x2d, wr, br, wg, wu, wd, g, b, tm):
    n = x2d.shape[0]
    n_tiles = n // tm + N_GROUPS - 1
    gate, gid, rank, cnt = _moe_route(x2d, wr, br, tm)
    cnt = cnt[:N_GROUPS, 0].astype(jnp.int32)
    tiles_g = (cnt + tm - 1) // tm
    tile_end = jnp.cumsum(tiles_g)
    dest = ((tile_end - tiles_g) * tm)[gid.reshape(n)] + rank.reshape(n)
    n_used = tile_end[-1]
    tile_id = jnp.arange(n_tiles, dtype=jnp.int32)
    used = (tile_id < n_used).astype(jnp.int32)
    in_blk = jnp.minimum(tile_id, n_used - 1)
    grp = jnp.sum((in_blk[:, None] >= tile_end[None, :]).astype(jnp.int32), axis=1)
    rows = _moe_scatter(dest, x2d, gate, n_tiles * tm, tm)
    ys = _moe_experts(grp, in_blk, used, rows, wg, wu, wd, tm)
    return _moe_gather_ln(dest, x2d, ys, g, b, tm)


def _attn_weights(w_in):
    w = ATTN_WIDTH
    qi_w = IDX_HEADS * IDX_DIM
    wq, wk, wv = w_in[:, :w], w_in[:, w:2 * w], w_in[:, 2 * w:3 * w]
    wqi = w_in[:, 3 * w:3 * w + qi_w]
    wki = w_in[:, 3 * w + qi_w:3 * w + qi_w + IDX_DIM]
    wwi = w_in[:, 3 * w + qi_w + IDX_DIM:]
    pad = jnp.zeros((D_MODEL, LANES - IDX_HEADS), w_in.dtype)
    widx = jnp.concatenate([wqi, wki, wki, wwi, pad], axis=1)
    return [a.astype(MXU_DTYPE) for a in (wq, wk, wv, widx)]


def _attn_groups(q_dtype):
    half = ATTN_HEAD_DIM // 2
    qi_w = IDX_HEADS * IDX_DIM
    qk_scale = ATTN_HEAD_DIM ** -0.5
    idx_scale = IDX_DIM ** -0.5
    return (
        (half, ATTN_WIDTH, ((0, ATTN_WIDTH, qk_scale, q_dtype),)),
        (half, ATTN_WIDTH, ((0, ATTN_WIDTH, 1.0, F32), (0, ATTN_WIDTH, 1.0, MXU_DTYPE))),
        (0, 0, ((0, ATTN_WIDTH, 1.0, F32), (0, ATTN_WIDTH, 1.0, MXU_DTYPE))),
        (half, qi_w + LANES, ((0, qi_w, idx_scale, q_dtype),
                              (qi_w, IDX_DIM, 1.0, F32),
                              (qi_w, LANES, 1.0, MXU_DTYPE),
                              (qi_w + LANES, LANES, IDX_HEADS ** -0.5, F32))),
    )


def _ret_weights(w_in):
    hq, hv = RET_HEADS * RET_QK_DIM, RET_HEADS * RET_V_DIM
    parts = (w_in[:, :hq], w_in[:, hq:2 * hq], w_in[:, 2 * hq:2 * hq + hv], w_in[:, 2 * hq + hv:])
    return [a.astype(MXU_DTYPE) for a in parts]


def _ret_groups():
    half = RET_QK_DIM // 2
    hq, hv = RET_HEADS * RET_QK_DIM, RET_HEADS * RET_V_DIM
    return (
        (half, hq, ((0, hq, 1.0, F32),)),
        (half, hq, ((0, hq, RET_QK_DIM ** -0.5, F32),)),
        (0, 0, ((0, hv, 1.0, F32),)),
        (0, 0, ((0, hv, 1.0, F32),)),
    )


def kernel(x_prompt, x_sample, cache_k, cache_v, cache_idx_k, state_ret, page_table, w_in_attn, w_out_attn, w_in_ret, ret_norm_gain, w_out_ret, w_router, b_router, w_exp_gate, w_exp_up, w_exp_down, ln_gain, ln_bias):
    batch, t_len, _ = x_prompt.shape
    dec_batch, ts, _ = x_sample.shape
    n_pages = page_table.shape[1]
    past = n_pages * PAGE_SIZE
    pool = cache_k.shape[1]
    n_p, n_s = batch * t_len, dec_batch * ts
    topk_p = min(IDX_TOPK_MAX, t_len // 4)
    topk_s = min(IDX_TOPK_MAX, (past + ts) // 4)
    tm_p = _token_tile(t_len, 512)
    tm_s = _token_tile(n_s, 512)
    pos_p = jnp.arange(t_len, dtype=jnp.int32)
    pos_s = past + (jnp.arange(tm_s, dtype=jnp.int32) % ts)

    xp = x_prompt.reshape(n_p, D_MODEL)
    xs = x_sample.reshape(n_s, D_MODEL)

    wr = jnp.concatenate([w_router, jnp.zeros((D_MODEL, LANES - N_EXPERTS), F32)], axis=1)
    br = jnp.concatenate([b_router, jnp.zeros((LANES - N_EXPERTS,), F32)])[None, :]

    def moe(x2d, i, tm, impl):
        return impl(x2d, wr, br, w_exp_gate[i].astype(MXU_DTYPE), w_exp_up[i].astype(MXU_DTYPE),
                    w_exp_down[i].astype(MXU_DTYPE), ln_gain[i, 1][None, :], ln_bias[i, 1][None, :], tm)

    a = 0
    aw = _attn_weights(w_in_attn[a])
    w_out = w_out_attn[a].astype(MXU_DTYPE)
    g0, b0 = ln_gain[0, 0][None, :], ln_bias[0, 0][None, :]

    cos_p, sin_p = _rope_tables(pos_p, ATTN_HEAD_DIM, ATTN_WIDTH)
    q_p, k_p, kb_p, v_p, vb_p, qi_p, ki_p, kk_p, wi_p = _project(
        xp, aw, _attn_groups(MXU_DTYPE), cos_p, sin_p, t_len // tm_p, tm_p)
    bias_p = _prompt_index_mask(qi_p, wi_p, kk_p, batch, t_len, topk_p)
    o_p = _prompt_attention(q_p, bias_p, kb_p, vb_p, batch, t_len)
    xp = _outproj_ln(o_p, xp, w_out, g0, b0, tm_p)

    cos_s, sin_s = _rope_tables(pos_s, ATTN_HEAD_DIM, ATTN_WIDTH)
    q_s, k_s, _, v_s, _, qi_s, ki_s, _, wi_s = _project(
        xs, aw, _attn_groups(F32), cos_s, sin_s, 1, tm_s)
    cache_ikt = jnp.transpose(cache_idx_k[a], (0, 2, 1))
    scores_s = _sample_scores(page_table, qi_s, wi_s, ki_s, cache_ikt, dec_batch, ts)
    bias_s = _sample_mask(scores_s.reshape(n_s, past + LANES), topk_s).reshape(dec_batch, ts, past + LANES)
    cache_kt = jnp.transpose(cache_k[a], (0, 2, 3, 1)).reshape(pool, ATTN_WIDTH, PAGE_SIZE)
    cache_vt = jnp.transpose(cache_v[a], (0, 2, 3, 1)).reshape(pool, ATTN_WIDTH, PAGE_SIZE)
    o_s = _sample_attention(page_table, q_s, bias_s, k_s, v_s, cache_kt, cache_vt, dec_batch, ts)
    xs = _outproj_ln(o_s, xs, w_out, g0, b0, tm_s)

    new_k_p = k_p.reshape(1, batch, t_len, ATTN_HEADS, ATTN_HEAD_DIM)
    new_v_p = v_p.reshape(1, batch, t_len, ATTN_HEADS, ATTN_HEAD_DIM)
    new_ik_p = ki_p.reshape(1, batch, t_len, IDX_DIM)
    new_k_s = k_s.reshape(1, dec_batch, ts, ATTN_HEADS, ATTN_HEAD_DIM)
    new_v_s = v_s.reshape(1, dec_batch, ts, ATTN_HEADS, ATTN_HEAD_DIM)
    new_ik_s = ki_s.reshape(1, dec_batch, ts, IDX_DIM)

    xp = moe(xp, 0, _token_tile(n_p, 1024), _moe_sorted_ln)
    xs = moe(xs, 0, _token_tile(n_s, 1024), _moe_ln)

    r = 0
    rw = _ret_weights(w_in_ret[r])
    w_out_r = w_out_ret[r].astype(MXU_DTYPE)
    gain = ret_norm_gain[r][None, :]
    g1, b1 = ln_gain[1, 0][None, :], ln_bias[1, 0][None, :]
    hq = RET_HEADS * RET_QK_DIM
    tm_rp = _token_tile(t_len, 256)
    tm_rs = _token_tile(n_s, 256)

    cos_p, sin_p = _rope_tables(pos_p, RET_QK_DIM, hq)
    rq, rk, rv, rg = _project(xp, rw, _ret_groups(), cos_p, sin_p, t_len // tm_rp, tm_rp)
    state0 = jnp.zeros((batch, RET_HEADS, RET_QK_DIM, RET_V_DIM), F32)
    ro, state_p = _retention_scan(rq, rk, rv, state0, batch, t_len // RET_CHUNK, RET_CHUNK)
    xp = _retention_output_ln(ro, rg, gain, xp, w_out_r, g1, b1, tm_rp)

    cos_s, sin_s = _rope_tables(past + (jnp.arange(tm_rs, dtype=jnp.int32) % ts), RET_QK_DIM, hq)
    rq, rk, rv, rg = _project(xs, rw, _ret_groups(), cos_s, sin_s, 1, tm_rs)
    ro, state_s = _retention_scan(rq, rk, rv, state_ret[r], dec_batch, 1, ts)
    xs = _retention_output_ln(ro, rg, gain, xs, w_out_r, g1, b1, tm_rs)

    xp = moe(xp, 1, _token_tile(n_p, 1024), _moe_sorted_ln)
    xs = moe(xs, 1, _token_tile(n_s, 1024), _moe_ln)

    return (xp.reshape(batch, t_len, D_MODEL), xs.reshape(dec_batch, ts, D_MODEL),
            new_k_p, new_v_p, new_ik_p, new_k_s, new_v_s, new_ik_s,
            state_p[None], state_s[None])
```

```python
import functools

import jax
import jax.numpy as jnp
from jax import lax
from jax.experimental import pallas as pl
from jax.experimental.pallas import tpu as pltpu

F32 = jnp.float32
BF16 = jnp.bfloat16
MXU_DTYPE = jnp.bfloat16

D_MODEL = 1024
PAGE_SIZE = 128
ATTN_HEADS = 16
ATTN_HEAD_DIM = 64
ATTN_WIDTH = ATTN_HEADS * ATTN_HEAD_DIM
IDX_HEADS = 8
IDX_DIM = 64
IDX_TOPK_MAX = 256
Q_BLOCK = 128
ROPE_THETA = 10000.0
RET_HEADS = 4
RET_QK_DIM = D_MODEL // RET_HEADS
RET_V_DIM = 2 * D_MODEL // RET_HEADS
RET_CHUNK = 128
N_EXPERTS = 16
N_GROUPS = 4
EXPERTS_PER_GROUP = N_EXPERTS // N_GROUPS
EXPERT_DIM = D_MODEL // 2
DEPTH = 2
DEEPNORM_ALPHA = (2 * DEPTH) ** 0.25
LN_EPS = 1e-5

LANES = 128
SUBLANES = 8
VMEM_LIMIT_BYTES = 58 * 1024 * 1024

NEG_INF = float("-inf")


def _cparams(*sem):
    return pltpu.CompilerParams(dimension_semantics=sem, vmem_limit_bytes=VMEM_LIMIT_BYTES)


def _mm(a, b):
    return jnp.dot(a, b, preferred_element_type=F32)


def _mm_nt(a, b):
    return lax.dot_general(a, b, (((1,), (1,)), ((), ())), preferred_element_type=F32)


def _mm_tn(a, b):
    return lax.dot_general(a, b, (((0,), (0,)), ((), ())), preferred_element_type=F32)


def _layer_norm(z, g, b):
    mu = jnp.mean(z, axis=-1, keepdims=True)
    d = z - mu
    var = jnp.mean(d * d, axis=-1, keepdims=True)
    return d * lax.rsqrt(var + LN_EPS) * g + b


def _sigmoid(x):
    return 1.0 / (1.0 + jnp.exp(-x))


def _rope_tables(pos, head_dim, width):
    half = head_dim // 2
    inv = ROPE_THETA ** (-jnp.arange(0, head_dim, 2, dtype=F32) / head_dim)
    ang = pos.astype(F32)[:, None] * inv[None, :]
    cos, sin = jnp.cos(ang), jnp.sin(ang)
    cos_h = jnp.concatenate([cos, cos], axis=1)
    sin_h = jnp.concatenate([-sin, sin], axis=1)
    reps = width // head_dim
    return jnp.tile(cos_h, (1, reps)), jnp.tile(sin_h, (1, reps))


def _rope(y, cos, sin, half):
    w = y.shape[-1]
    lane = lax.broadcasted_iota(jnp.int32, y.shape, 1)
    first = (lane % (2 * half)) < half
    swapped = jnp.where(first, pltpu.roll(y, w - half, 1), pltpu.roll(y, half, 1))
    return y * cos + swapped * sin


def _proj_kernel(x_ref, cos_ref, sin_ref, *refs, groups):
    n_g = len(groups)
    w_refs, out_refs = refs[:n_g], refs[n_g:]
    xb = x_ref[...].astype(MXU_DTYPE)
    oi = 0
    for g, (rope_half, rope_w, outs) in enumerate(groups):
        y = _mm(xb, w_refs[g][...])
        if rope_half:
            yr = _rope(y[:, :rope_w], cos_ref[:, :rope_w], sin_ref[:, :rope_w], rope_half)
            y = yr if rope_w == y.shape[1] else jnp.concatenate([yr, y[:, rope_w:]], axis=1)
        for (c0, w, scale, dt) in outs:
            piece = y[:, c0:c0 + w]
            if scale != 1.0:
                piece = piece * scale
            out_refs[oi][...] = piece.astype(dt)
            oi += 1


def _token_tile(n, cap):
    tm = min(cap, n)
    assert n % tm == 0 and tm % SUBLANES == 0
    return tm


def _project(x2d, weights, groups, cos, sin, n_tab_blocks, tm):
    n = x2d.shape[0]
    out_shapes, out_specs = [], []
    for (_, _, outs) in groups:
        for (_, w, _, dt) in outs:
            out_shapes.append(jax.ShapeDtypeStruct((n, w), dt))
            out_specs.append(pl.BlockSpec((tm, w), lambda m: (m, 0)))
    tw = cos.shape[1]
    in_specs = [pl.BlockSpec((tm, D_MODEL), lambda m: (m, 0)),
                pl.BlockSpec((tm, tw), lambda m: (m % n_tab_blocks, 0)),
                pl.BlockSpec((tm, tw), lambda m: (m % n_tab_blocks, 0))]
    for w in weights:
        in_specs.append(pl.BlockSpec(w.shape, lambda m: (0, 0)))
    return pl.pallas_call(
        functools.partial(_proj_kernel, groups=groups),
        grid=(n // tm,),
        in_specs=in_specs,
        out_specs=out_specs,
        out_shape=out_shapes,
        compiler_params=_cparams("parallel"),
        name="proj",
    )(x2d, cos, sin, *weights)


LOG2_E = 1.4426950408889634


def _rope_t(yt, cos_t, sin_t, head_dim):
    rows, m = yt.shape
    half = head_dim // 2
    y3 = yt.reshape(rows // head_dim, head_dim, m)
    x1, x2 = y3[:, :half, :], y3[:, half:, :]
    out = jnp.concatenate([x1 * cos_t - x2 * sin_t, x1 * sin_t + x2 * cos_t], axis=1)
    return out.reshape(rows, m)


def _proj_prompt_attn_kernel(x_ref, cos_ref, sin_ref, cos_t_ref, sin_t_ref,
                             wq_ref, widx_ref, wkt_ref, wvt_ref, wkit_ref,
                             q_ref, qi_ref, wi_ref, kt_ref, ktb_ref, vt_ref, vtb_ref, kit_ref, kktb_ref):
    xb = x_ref[...].astype(MXU_DTYPE)
    half = ATTN_HEAD_DIM // 2
    qi_w = IDX_HEADS * IDX_DIM
    q = _rope(_mm(xb, wq_ref[...]), cos_ref[...], sin_ref[...], half)
    q_ref[...] = (q * (ATTN_HEAD_DIM ** -0.5 * LOG2_E)).astype(q_ref.dtype)
    yi = _mm(xb, widx_ref[...])
    qi = _rope(yi[:, :qi_w], cos_ref[:, :qi_w], sin_ref[:, :qi_w], half)
    qi_ref[...] = (qi * IDX_DIM ** -0.5).astype(qi_ref.dtype)
    wi_ref[...] = yi[:, qi_w:] * IDX_HEADS ** -0.5
    cos_t, sin_t = cos_t_ref[...], sin_t_ref[...]
    kt = _rope_t(_mm_nt(wkt_ref[...], xb), cos_t, sin_t, ATTN_HEAD_DIM)
    kt_ref[...] = kt.reshape(kt_ref.shape)
    ktb_ref[...] = kt.astype(ktb_ref.dtype)
    vt = _mm_nt(wvt_ref[...], xb)
    vt_ref[...] = vt.reshape(vt_ref.shape)
    vtb_ref[...] = vt.astype(vtb_ref.dtype)
    kit = _rope_t(_mm_nt(wkit_ref[...], xb), cos_t, sin_t, IDX_DIM)
    kit_ref[...] = kit
    kib = kit.astype(kktb_ref.dtype)
    kktb_ref[...] = jnp.concatenate([kib, kib], axis=0)


def _project_prompt_attn(x2d, w_in, pos, batch, t_len, tm):
    n = x2d.shape[0]
    nt = t_len // tm
    w = ATTN_WIDTH
    qi_w = IDX_HEADS * IDX_DIM
    cos, sin = _rope_tables(pos, ATTN_HEAD_DIM, w)
    half = ATTN_HEAD_DIM // 2
    cos_t, sin_t = jnp.transpose(cos[:, :half]), jnp.transpose(sin[:, half:ATTN_HEAD_DIM])
    pad = jnp.zeros((D_MODEL, LANES - IDX_HEADS), w_in.dtype)
    weights = [w_in[:, :w],
               jnp.concatenate([w_in[:, 3 * w:3 * w + qi_w], w_in[:, 3 * w + qi_w + IDX_DIM:], pad], axis=1),
               jnp.transpose(w_in[:, w:2 * w]),
               jnp.transpose(w_in[:, 2 * w:3 * w]),
               jnp.transpose(w_in[:, 3 * w + qi_w:3 * w + qi_w + IDX_DIM])]
    weights = [a.astype(MXU_DTYPE) for a in weights]
    heads_t = (batch, ATTN_HEADS, ATTN_HEAD_DIM, t_len)
    out_shape = [jax.ShapeDtypeStruct((n, w), MXU_DTYPE),
                 jax.ShapeDtypeStruct((n, qi_w), MXU_DTYPE),
                 jax.ShapeDtypeStruct((n, LANES), F32),
                 jax.ShapeDtypeStruct(heads_t, F32),
                 jax.ShapeDtypeStruct((batch, w, t_len), MXU_DTYPE),
                 jax.ShapeDtypeStruct(heads_t, F32),
                 jax.ShapeDtypeStruct((batch, w, t_len), MXU_DTYPE),
                 jax.ShapeDtypeStruct((batch, IDX_DIM, t_len), F32),
                 jax.ShapeDtypeStruct((batch, 2 * IDX_DIM, t_len), MXU_DTYPE)]

    def rows(width):
        return pl.BlockSpec((tm, width), lambda m: (m, 0))

    def cols(height):
        return pl.BlockSpec((None, height, tm), lambda m: (m // nt, 0, m % nt))

    heads_spec = pl.BlockSpec((None, ATTN_HEADS, ATTN_HEAD_DIM, tm), lambda m: (m // nt, 0, 0, m % nt))
    out_specs = [rows(w), rows(qi_w), rows(LANES), heads_spec, cols(w), heads_spec, cols(w),
                 cols(IDX_DIM), cols(2 * IDX_DIM)]
    in_specs = [rows(D_MODEL),
                pl.BlockSpec((tm, w), lambda m: (m % nt, 0)),
                pl.BlockSpec((tm, w), lambda m: (m % nt, 0)),
                pl.BlockSpec((half, tm), lambda m: (0, m % nt)),
                pl.BlockSpec((half, tm), lambda m: (0, m % nt))]
    in_specs += [pl.BlockSpec(a.shape, lambda m: (0, 0)) for a in weights]
    return pl.pallas_call(
        _proj_prompt_attn_kernel,
        grid=(n // tm,),
        in_specs=in_specs,
        out_specs=out_specs,
        out_shape=out_shape,
        compiler_params=_cparams("parallel"),
        name="proj_prompt_attn",
    )(x2d, cos, sin, cos_t, sin_t, *weights)


def _row_count(pred):
    return jnp.sum(jnp.where(pred, 1.0, 0.0), axis=1, keepdims=True)


KEY_NEG_INF = -2139095041


def _key_to_float(key):
    bits = key ^ ((key >> 31) & jnp.int32(0x7FFFFFFF))
    return jnp.where(key <= KEY_NEG_INF, NEG_INF, lax.bitcast_convert_type(bits, F32))


def _topk_select(score, k):
    r, s = score.shape
    kf = jnp.float32(k)

    def value_bit(i, carry):
        t, cnt = carry
        cand = t + lax.shift_left(jnp.int32(1), jnp.int32(31) - i)
        c = _row_count(score >= _key_to_float(cand))
        ok = c >= kf
        return jnp.where(ok, cand, t), jnp.where(ok, c, cnt)

    t, c_ge = lax.fori_loop(0, 32, value_bit,
                            (jnp.full((r, 1), jnp.iinfo(jnp.int32).min, jnp.int32),
                             jnp.full((r, 1), float(s), F32)))
    def lowest(mask):
        v = jnp.min(jnp.where(mask, score, jnp.inf), axis=1, keepdims=True)
        return v, _row_count(score == v)

    thr, n_eq = lowest(score >= _key_to_float(t))

    def step_up(carry):
        thr, n_eq, c_ge = carry
        for _ in range(3):
            more = (c_ge - n_eq) >= kf
            above, n_above = lowest(score > thr)
            thr = jnp.where(more, above, thr)
            c_ge = jnp.where(more, c_ge - n_eq, c_ge)
            n_eq = jnp.where(more, n_above, n_eq)
        return thr, n_eq, c_ge

    thr, n_eq, c_ge = lax.cond(jnp.max(c_ge - n_eq) >= kf, step_up, lambda carry: carry, (thr, n_eq, c_ge))
    gt = score > thr
    eq = score == thr
    need = kf - (c_ge - n_eq)
    col = lax.broadcasted_iota(jnp.int32, (r, s), 1)
    n_bits = max(1, (s - 1).bit_length())

    def last_tie_column():
        def index_bit(i, j):
            cand = j + lax.shift_left(jnp.int32(1), jnp.int32(n_bits - 1) - i)
            return jnp.where(_row_count(eq & (col < cand)) < need, cand, j)
        return lax.fori_loop(0, n_bits, index_bit, jnp.zeros((r, 1), jnp.int32))

    j = lax.cond(jnp.max(n_eq - need) > 0.0, last_tie_column, lambda: jnp.full((r, 1), s, jnp.int32))
    keep = gt | (eq & (col <= j))
    return keep & (score > NEG_INF)


def _causal_widths(t_len):
    step = min(2 * Q_BLOCK, t_len)
    assert t_len % step == 0
    return tuple(range(step, t_len + 1, step))


def _for_causal_width(n_keys, widths, run, enable=True):
    prev = 0
    for w in widths:
        pl.when(enable & (n_keys > prev) & (n_keys <= w))(functools.partial(run, w))
        prev = w


def _idx_mask_kernel(qi_ref, wi_ref, kk_ref, bias_ref, *, topk, widths):
    i = pl.program_id(1)
    qb, t_len = qi_ref.shape[0], kk_ref.shape[1]
    n_keys = (i + 1) * qb
    few = n_keys <= topk

    @pl.when(few)
    def _():
        row = i * qb + lax.broadcasted_iota(jnp.int32, (qb, t_len), 0)
        col = lax.broadcasted_iota(jnp.int32, (qb, t_len), 1)
        bias_ref[...] = jnp.where(col <= row, 0.0, NEG_INF).astype(bias_ref.dtype)

    def run(w):
        qi = qi_ref[...]
        kk = kk_ref[:, :w]
        wi = wi_ref[...]
        lane = lax.broadcasted_iota(jnp.int32, (qb, LANES), 1)
        zero = jnp.zeros((qb, LANES), qi.dtype)
        score = jnp.zeros((qb, w), F32)
        for p in range(IDX_HEADS // 2):
            qp = qi[:, LANES * p:LANES * (p + 1)]
            for half in range(2):
                lhs = jnp.where(lane < IDX_DIM, qp, zero) if half == 0 else jnp.where(lane >= IDX_DIM, qp, zero)
                h = 2 * p + half
                score = score + jnp.maximum(_mm(lhs, kk), 0.0) * wi[:, h:h + 1]
        row = i * qb + lax.broadcasted_iota(jnp.int32, (qb, w), 0)
        col = lax.broadcasted_iota(jnp.int32, (qb, w), 1)
        score = jnp.where(col <= row, score, NEG_INF)
        keep = _topk_select(score, topk)
        bias_ref[:, :w] = jnp.where(keep, 0.0, NEG_INF).astype(bias_ref.dtype)
        if w < t_len:
            bias_ref[:, w:] = jnp.full((qb, t_len - w), NEG_INF, bias_ref.dtype)

    _for_causal_width(n_keys, widths, run, enable=jnp.logical_not(few))


def _prompt_index_mask(qi, wi, kk, batch, t_len, topk):
    nb = t_len // Q_BLOCK
    return pl.pallas_call(
        functools.partial(_idx_mask_kernel, topk=topk, widths=_causal_widths(t_len)),
        grid=(batch, nb),
        in_specs=[pl.BlockSpec((Q_BLOCK, qi.shape[1]), lambda b, i: (b * nb + i, 0)),
                  pl.BlockSpec((Q_BLOCK, LANES), lambda b, i: (b * nb + i, 0)),
                  pl.BlockSpec((None, LANES, t_len), lambda b, i: (b, 0, 0))],
        out_specs=pl.BlockSpec((Q_BLOCK, t_len), lambda b, i: (b * nb + i, 0)),
        out_shape=jax.ShapeDtypeStruct((batch * t_len, t_len), BF16),
        compiler_params=_cparams("parallel", "arbitrary"),
        name="prompt_index_mask",
    )(qi, wi, kk)


def _attn_kernel(q_ref, bias_ref, k_ref, v_ref, o_ref, *, widths):
    i = pl.program_id(1)
    qb = q_ref.shape[0]

    def run(w):
        bias = bias_ref[:, :w].astype(F32)
        bias2 = jnp.concatenate([bias, bias], axis=0)
        lane = lax.broadcasted_iota(jnp.int32, (qb, LANES), 1)
        low = lane < ATTN_HEAD_DIM
        zero = jnp.zeros((qb, LANES), q_ref.dtype)
        for p in range(ATTN_HEADS // 2):
            sl = slice(LANES * p, LANES * (p + 1))
            qp = q_ref[:, sl]
            lhs = jnp.concatenate([jnp.where(low, qp, zero), jnp.where(low, zero, qp)], axis=0)
            s = _mm(lhs, k_ref[sl, :w]) + bias2
            m = jnp.max(s, axis=1, keepdims=True)
            e = jnp.exp2(s - m)
            l = jnp.sum(e, axis=1, keepdims=True)
            pv = _mm_nt(e.astype(MXU_DTYPE), v_ref[sl, :w]) / l
            o_ref[:, sl] = jnp.where(low, pv[:qb], pv[qb:]).astype(o_ref.dtype)

    _for_causal_width((i + 1) * qb, widths, run)


def _prompt_attention(q, bias, k, v, batch, t_len):
    nb = t_len // Q_BLOCK
    return pl.pallas_call(
        functools.partial(_attn_kernel, widths=_causal_widths(t_len)),
        grid=(batch, nb),
        in_specs=[pl.BlockSpec((Q_BLOCK, ATTN_WIDTH), lambda b, i: (b * nb + i, 0)),
                  pl.BlockSpec((Q_BLOCK, t_len), lambda b, i: (b * nb + i, 0)),
                  pl.BlockSpec((None, ATTN_WIDTH, t_len), lambda b, i: (b, 0, 0)),
                  pl.BlockSpec((None, ATTN_WIDTH, t_len), lambda b, i: (b, 0, 0))],
        out_specs=pl.BlockSpec((Q_BLOCK, ATTN_WIDTH), lambda b, i: (b * nb + i, 0)),
        out_shape=jax.ShapeDtypeStruct((batch * t_len, ATTN_WIDTH), MXU_DTYPE),
        compiler_params=_cparams("parallel", "arbitrary"),
        name="prompt_attention",
    )(q, bias, k, v)


def _sample_score_kernel(pt_ref, qi_ref, wi_ref, kin_ref, *refs, n_pages):
    del pt_ref
    page_refs, out_ref = refs[:n_pages], refs[n_pages]
    ts = qi_ref.shape[0]
    qi = qi_ref[...]
    wi = wi_ref[...]
    lhs = jnp.concatenate([qi[:, IDX_DIM * h:IDX_DIM * (h + 1)] for h in range(IDX_HEADS)],
                          axis=0).astype(MXU_DTYPE)
    wcol = jnp.concatenate([wi[:, h:h + 1] for h in range(IDX_HEADS)], axis=0)

    def head_sum(s):
        w = jnp.maximum(s, 0.0) * wcol
        acc = w[0:ts]
        for h in range(1, IDX_HEADS):
            acc = acc + w[ts * h:ts * (h + 1)]
        return acc

    for j in range(n_pages):
        kpt = page_refs[j][0].astype(MXU_DTYPE)
        out_ref[0, :, PAGE_SIZE * j:PAGE_SIZE * (j + 1)] = head_sum(_mm(lhs, kpt))
    kn = kin_ref[...].astype(MXU_DTYPE)
    kn = jnp.concatenate([kn, jnp.zeros((LANES - ts, IDX_DIM), MXU_DTYPE)], axis=0)
    sn = head_sum(_mm_nt(lhs, kn))
    row = lax.broadcasted_iota(jnp.int32, sn.shape, 0)
    col = lax.broadcasted_iota(jnp.int32, sn.shape, 1)
    past = PAGE_SIZE * n_pages
    out_ref[0, :, past:past + LANES] = jnp.where(col <= row, sn, NEG_INF)


def _sample_scores(page_table, qi, wi, ki_new, cache_ikt, dec_batch, ts):
    n_pages = page_table.shape[1]
    past = n_pages * PAGE_SIZE

    def page_spec(j):
        return pl.BlockSpec((1, IDX_DIM, PAGE_SIZE), lambda b, pt: (pt[b, j], 0, 0))

    grid_spec = pltpu.PrefetchScalarGridSpec(
        num_scalar_prefetch=1,
        grid=(dec_batch,),
        in_specs=[pl.BlockSpec((ts, qi.shape[1]), lambda b, pt: (b, 0)),
                  pl.BlockSpec((ts, LANES), lambda b, pt: (b, 0)),
                  pl.BlockSpec((ts, IDX_DIM), lambda b, pt: (b, 0))]
                 + [page_spec(j) for j in range(n_pages)],
        out_specs=pl.BlockSpec((1, ts, past + LANES), lambda b, pt: (b, 0, 0)),
    )
    return pl.pallas_call(
        functools.partial(_sample_score_kernel, n_pages=n_pages),
        grid_spec=grid_spec,
        out_shape=jax.ShapeDtypeStruct((dec_batch, ts, past + LANES), F32),
        compiler_params=_cparams("arbitrary"),
        name="sample_scores",
    )(page_table, qi, wi, ki_new, *([cache_ikt] * n_pages))


def _mask_kernel(score_ref, bias_ref, *, topk):
    keep = _topk_select(score_ref[...], topk)
    bias_ref[...] = jnp.where(keep, 0.0, NEG_INF).astype(bias_ref.dtype)


def _sample_mask(scores2d, topk):
    n, s = scores2d.shape
    rb = _token_tile(n, Q_BLOCK)
    return pl.pallas_call(
        functools.partial(_mask_kernel, topk=topk),
        grid=(n // rb,),
        in_specs=[pl.BlockSpec((rb, s), lambda i: (i, 0))],
        out_specs=pl.BlockSpec((rb, s), lambda i: (i, 0)),
        out_shape=jax.ShapeDtypeStruct((n, s), BF16),
        compiler_params=_cparams("parallel"),
        name="sample_mask",
    )(scores2d)


def _head_lane_mask(shape, h):
    lane = lax.broadcasted_iota(jnp.int32, shape, 1)
    return (lane >= ATTN_HEAD_DIM * h) & (lane < ATTN_HEAD_DIM * (h + 1))


def _sample_attn_kernel(pt_ref, q_ref, bias_ref, kn_ref, vn_ref, *refs, n_pages):
    del pt_ref
    k_pages, v_pages, o_ref = refs[:n_pages], refs[n_pages:2 * n_pages], refs[2 * n_pages]
    ts = q_ref.shape[0]
    past = n_pages * PAGE_SIZE
    q = q_ref[...]
    qbd = jnp.concatenate([jnp.where(_head_lane_mask(q.shape, h), q, 0.0)
                           for h in range(ATTN_HEADS)], axis=0).astype(MXU_DTYPE)
    pad = jnp.zeros((LANES - ts, ATTN_WIDTH), F32)
    kn = jnp.concatenate([kn_ref[...], pad], axis=0).astype(MXU_DTYPE)
    vn = jnp.concatenate([vn_ref[...], pad], axis=0).astype(MXU_DTYPE)
    s = jnp.concatenate([_mm(qbd, k_pages[pg][0].astype(MXU_DTYPE)) for pg in range(n_pages)]
                        + [_mm_nt(qbd, kn)], axis=1)
    s = s + jnp.concatenate([bias_ref[0].astype(F32)] * ATTN_HEADS, axis=0)
    m = jnp.max(s, axis=1, keepdims=True)
    e = jnp.exp(s - m)
    l = jnp.sum(e, axis=1, keepdims=True)
    eb = e.astype(MXU_DTYPE)
    pv = _mm(eb[:, past:], vn)
    for pg in range(n_pages):
        pv = pv + _mm_nt(eb[:, PAGE_SIZE * pg:PAGE_SIZE * (pg + 1)], v_pages[pg][0].astype(MXU_DTYPE))
    out = pv / l
    o = jnp.zeros((ts, ATTN_WIDTH), F32)
    for h in range(ATTN_HEADS):
        o = o + jnp.where(_head_lane_mask(o.shape, h), out[ts * h:ts * (h + 1)], 0.0)
    o_ref[...] = o.astype(o_ref.dtype)


def _sample_attention(page_table, q, bias, k_new, v_new, cache_kt, cache_vt, dec_batch, ts):
    n_pages = page_table.shape[1]
    past = n_pages * PAGE_SIZE

    def page_spec(g):
        return pl.BlockSpec((1, ATTN_WIDTH, PAGE_SIZE), lambda b, pt: (pt[b, g], 0, 0))

    grid_spec = pltpu.PrefetchScalarGridSpec(
        num_scalar_prefetch=1,
        grid=(dec_batch,),
        in_specs=[pl.BlockSpec((ts, ATTN_WIDTH), lambda b, pt: (b, 0)),
                  pl.BlockSpec((1, ts, past + LANES), lambda b, pt: (b, 0, 0)),
                  pl.BlockSpec((ts, ATTN_WIDTH), lambda b, pt: (b, 0)),
                  pl.BlockSpec((ts, ATTN_WIDTH), lambda b, pt: (b, 0))]
                 + [page_spec(g) for g in range(n_pages)] * 2,
        out_specs=pl.BlockSpec((ts, ATTN_WIDTH), lambda b, pt: (b, 0)),
    )
    return pl.pallas_call(
        functools.partial(_sample_attn_kernel, n_pages=n_pages),
        grid_spec=grid_spec,
        out_shape=jax.ShapeDtypeStruct((dec_batch * ts, ATTN_WIDTH), MXU_DTYPE),
        compiler_params=_cparams("parallel"),
        name="sample_attention",
    )(page_table, q, bias, k_new, v_new, *([cache_kt] * n_pages), *([cache_vt] * n_pages))


def _outproj_ln_kernel(o_ref, x_ref, w_ref, g_ref, b_ref, y_ref):
    m = _mm(o_ref[...].astype(MXU_DTYPE), w_ref[...])
    y_ref[...] = _layer_norm(DEEPNORM_ALPHA * x_ref[...] + m, g_ref[...], b_ref[...])


def _outproj_ln(o, x2d, w, g, b, tm):
    n = x2d.shape[0]
    return pl.pallas_call(
        _outproj_ln_kernel,
        grid=(n // tm,),
        in_specs=[pl.BlockSpec((tm, o.shape[1]), lambda m: (m, 0)),
                  pl.BlockSpec((tm, D_MODEL), lambda m: (m, 0)),
                  pl.BlockSpec(w.shape, lambda m: (0, 0)),
                  pl.BlockSpec((1, D_MODEL), lambda m: (0, 0)),
                  pl.BlockSpec((1, D_MODEL), lambda m: (0, 0))],
        out_specs=pl.BlockSpec((tm, D_MODEL), lambda m: (m, 0)),
        out_shape=jax.ShapeDtypeStruct((n, D_MODEL), F32),
        compiler_params=_cparams("parallel"),
        name="outproj_ln",
    )(o, x2d, w, g, b)


def _retention_tables(chunk):
    lg = jnp.log(1.0 - 2.0 ** (-5.0 - jnp.arange(RET_HEADS, dtype=F32)))
    i = jnp.arange(chunk, dtype=F32)
    diff = i[:, None] - i[None, :]
    dmask = jnp.where(diff[None] >= 0, jnp.exp(jnp.maximum(diff, 0.0)[None] * lg[:, None, None]), 0.0)
    cross = jnp.exp((i + 1.0)[:, None] * lg[None, :])
    kdec = jnp.exp((chunk - 1.0 - i)[:, None] * lg[None, :])
    cdec = jnp.exp(chunk * lg)
    pad = jnp.zeros((chunk, LANES - RET_HEADS), F32)
    cross = jnp.concatenate([cross, pad], axis=1)
    kdec = jnp.concatenate([kdec, pad], axis=1)
    cdec = jnp.broadcast_to(cdec[:, None, None], (RET_HEADS, 1, RET_V_DIM))
    return dmask.astype(F32), cross, kdec, cdec


def _retention_kernel(q_ref, k_ref, v_ref, s0_ref, dmask_ref, cross_ref, kdec_ref, cdec_ref,
                      o_ref, s_out_ref, state_sc):
    c = pl.program_id(1)

    @pl.when(c == 0)
    def _():
        state_sc[...] = s0_ref[0]

    cross = cross_ref[...]
    kdec = kdec_ref[...]
    for h in range(RET_HEADS):
        qs = slice(RET_QK_DIM * h, RET_QK_DIM * (h + 1))
        vs = slice(RET_V_DIM * h, RET_V_DIM * (h + 1))
        qh = q_ref[:, qs]
        kh = k_ref[:, qs]
        qb = qh.astype(MXU_DTYPE)
        vb = v_ref[:, vs].astype(MXU_DTYPE)
        state = state_sc[h]
        inner = _mm_nt(qb, kh.astype(MXU_DTYPE)) * dmask_ref[h]
        o = _mm(inner.astype(MXU_DTYPE), vb) + _mm(qb, state.astype(MXU_DTYPE)) * cross[:, h:h + 1]
        o_ref[:, vs] = o
        kd = (kh * kdec[:, h:h + 1]).astype(MXU_DTYPE)
        state_sc[h] = cdec_ref[h] * state + _mm_tn(kd, vb)

    @pl.when(c == pl.num_programs(1) - 1)
    def _():
        s_out_ref[0] = state_sc[...]


def _retention_scan(q, k, v, state0, batch, n_chunks, chunk):
    dmask, cross, kdec, cdec = _retention_tables(chunk)
    hq, hv = RET_HEADS * RET_QK_DIM, RET_HEADS * RET_V_DIM
    state_block = (1, RET_HEADS, RET_QK_DIM, RET_V_DIM)
    return pl.pallas_call(
        _retention_kernel,
        grid=(batch, n_chunks),
        in_specs=[pl.BlockSpec((chunk, hq), lambda b, c: (b * n_chunks + c, 0)),
                  pl.BlockSpec((chunk, hq), lambda b, c: (b * n_chunks + c, 0)),
                  pl.BlockSpec((chunk, hv), lambda b, c: (b * n_chunks + c, 0)),
                  pl.BlockSpec(state_block, lambda b, c: (b, 0, 0, 0)),
                  pl.BlockSpec(dmask.shape, lambda b, c: (0, 0, 0)),
                  pl.BlockSpec(cross.shape, lambda b, c: (0, 0)),
                  pl.BlockSpec(kdec.shape, lambda b, c: (0, 0)),
                  pl.BlockSpec(cdec.shape, lambda b, c: (0, 0, 0))],
        out_specs=[pl.BlockSpec((chunk, hv), lambda b, c: (b * n_chunks + c, 0)),
                   pl.BlockSpec(state_block, lambda b, c: (b, 0, 0, 0))],
        out_shape=[jax.ShapeDtypeStruct((batch * n_chunks * chunk, hv), F32),
                   jax.ShapeDtypeStruct((batch,) + state_block[1:], F32)],
        scratch_shapes=[pltpu.VMEM(state_block[1:], F32)],
        compiler_params=_cparams("parallel", "arbitrary"),
        name="retention_scan",
    )(q, k, v, state0, dmask, cross, kdec, cdec)


def _ret_out_kernel(o_ref, gate_ref, gain_ref, x_ref, w_ref, g_ref, b_ref, y_ref):
    pieces = []
    for h in range(RET_HEADS):
        vs = slice(RET_V_DIM * h, RET_V_DIM * (h + 1))
        of = o_ref[:, vs]
        mu = jnp.mean(of, axis=-1, keepdims=True)
        d = of - mu
        var = jnp.mean(d * d, axis=-1, keepdims=True)
        on = d * lax.rsqrt(var + LN_EPS) * gain_ref[:, vs]
        gt = gate_ref[:, vs]
        pieces.append(((gt * _sigmoid(gt)) * on).astype(MXU_DTYPE))
    m = _mm(jnp.concatenate(pieces, axis=1), w_ref[...])
    y_ref[...] = _layer_norm(DEEPNORM_ALPHA * x_ref[...] + m, g_ref[...], b_ref[...])


def _retention_output_ln(o, gate, gain, x2d, w, g, b, tm):
    n = x2d.shape[0]
    hv = RET_HEADS * RET_V_DIM
    return pl.pallas_call(
        _ret_out_kernel,
        grid=(n // tm,),
        in_specs=[pl.BlockSpec((tm, hv), lambda m: (m, 0)),
                  pl.BlockSpec((tm, hv), lambda m: (m, 0)),
                  pl.BlockSpec((1, hv), lambda m: (0, 0)),
                  pl.BlockSpec((tm, D_MODEL), lambda m: (m, 0)),
                  pl.BlockSpec(w.shape, lambda m: (0, 0)),
                  pl.BlockSpec((1, D_MODEL), lambda m: (0, 0)),
                  pl.BlockSpec((1, D_MODEL), lambda m: (0, 0))],
        out_specs=pl.BlockSpec((tm, D_MODEL), lambda m: (m, 0)),
        out_shape=jax.ShapeDtypeStruct((n, D_MODEL), F32),
        compiler_params=_cparams("parallel"),
        name="retention_out_ln",
    )(o, gate, gain, x2d, w, g, b)


def _route(logits, b_router):
    shape = logits.shape
    lane = lax.broadcasted_iota(jnp.int32, shape, 1)
    lane_f = lane.astype(F32)
    valid = lane < N_EXPERTS
    lg = jnp.where(valid, logits, NEG_INF)
    ex = jnp.exp(lg - jnp.max(lg, axis=1, keepdims=True))
    probs = ex / jnp.sum(ex, axis=1, keepdims=True)
    sel = probs + b_router
    far = jnp.float32(LANES)

    def first_max(v):
        m = jnp.max(v, axis=1, keepdims=True)
        idx = jnp.min(jnp.where(v == m, lane_f, far), axis=1, keepdims=True)
        return m, idx

    gscore = []
    for g in range(N_GROUPS):
        in_g = (lane >= EXPERTS_PER_GROUP * g) & (lane < EXPERTS_PER_GROUP * (g + 1))
        sg = jnp.where(in_g, sel, NEG_INF)
        m1, i1 = first_max(sg)
        m2 = jnp.max(jnp.where(lane_f == i1, NEG_INF, sg), axis=1, keepdims=True)
        gscore.append(m1 + m2)
    best, gbest = gscore[0], jnp.zeros_like(gscore[0])
    for g in range(1, N_GROUPS):
        better = gscore[g] > best
        best = jnp.where(better, gscore[g], best)
        gbest = jnp.where(better, jnp.float32(g), gbest)
    lo = gbest * EXPERTS_PER_GROUP
    in_best = (lane_f >= lo) & (lane_f < lo + EXPERTS_PER_GROUP)
    sm = jnp.where(in_best, sel, NEG_INF)
    _, e1 = first_max(sm)
    _, e2 = first_max(jnp.where(lane_f == e1, NEG_INF, sm))
    is1, is2 = lane_f == e1, lane_f == e2
    p1 = jnp.sum(jnp.where(is1, probs, 0.0), axis=1, keepdims=True)
    p2 = jnp.sum(jnp.where(is2, probs, 0.0), axis=1, keepdims=True)
    den = p1 + p2
    return jnp.where(is1, p1 / den, 0.0) + jnp.where(is2, p2 / den, 0.0), gbest


def _router_logits(x, wr_ref):
    xh = x.astype(MXU_DTYPE)
    xl = (x - xh.astype(F32)).astype(MXU_DTYPE)
    return _mm(xh, wr_ref[0]) + (_mm(xh, wr_ref[1]) + _mm(xl, wr_ref[0]))


def _expert_ffn(xb, gate, first_expert, wg_ref, wu_ref, wd_ref, n_experts):
    lane = lax.broadcasted_iota(jnp.int32, gate.shape, 1)
    acc = None
    for e in range(n_experts):
        ge = jnp.sum(jnp.where(lane == first_expert + e, gate, 0.0), axis=1, keepdims=True)
        hg = _mm(xb, wg_ref[e])
        hu = _mm(xb, wu_ref[e])
        h = (hg * _sigmoid(hg)) * hu * ge
        y = _mm(h.astype(MXU_DTYPE), wd_ref[e])
        acc = y if acc is None else acc + y
    return acc


def _moe_kernel(x_ref, wr_ref, br_ref, wg_ref, wu_ref, wd_ref, g_ref, b_ref, y_ref,
                xb_sc, gate_sc, acc_sc):
    e = pl.program_id(1)

    @pl.when(e == 0)
    def _():
        x = x_ref[...]
        xb_sc[...] = x.astype(MXU_DTYPE)
        gate_sc[...] = _route(_router_logits(x, wr_ref), br_ref[...])[0]
        acc_sc[...] = jnp.zeros(acc_sc.shape, F32)

    acc_sc[...] += _expert_ffn(xb_sc[...], gate_sc[...], e, wg_ref, wu_ref, wd_ref, 1)

    @pl.when(e == pl.num_programs(1) - 1)
    def _():
        y_ref[...] = _layer_norm(DEEPNORM_ALPHA * x_ref[...] + acc_sc[...], g_ref[...], b_ref[...])


def _moe_ln(x2d, wr, br, wg, wu, wd, g, b, tm):
    n = x2d.shape[0]
    return pl.pallas_call(
        _moe_kernel,
        grid=(n // tm, N_EXPERTS),
        in_specs=[pl.BlockSpec((tm, D_MODEL), lambda m, e: (m, 0)),
                  pl.BlockSpec(wr.shape, lambda m, e: (0, 0, 0)),
                  pl.BlockSpec(br.shape, lambda m, e: (0, 0)),
                  pl.BlockSpec((1, D_MODEL, EXPERT_DIM), lambda m, e: (e, 0, 0)),
                  pl.BlockSpec((1, D_MODEL, EXPERT_DIM), lambda m, e: (e, 0, 0)),
                  pl.BlockSpec((1, EXPERT_DIM, D_MODEL), lambda m, e: (e, 0, 0)),
                  pl.BlockSpec((1, D_MODEL), lambda m, e: (0, 0)),
                  pl.BlockSpec((1, D_MODEL), lambda m, e: (0, 0))],
        out_specs=pl.BlockSpec((tm, D_MODEL), lambda m, e: (m, 0)),
        out_shape=jax.ShapeDtypeStruct((n, D_MODEL), F32),
        scratch_shapes=[pltpu.VMEM((tm, D_MODEL), MXU_DTYPE),
                        pltpu.VMEM((tm, LANES), F32),
                        pltpu.VMEM((tm, D_MODEL), F32)],
        compiler_params=_cparams("parallel", "arbitrary"),
        name="moe_ln",
    )(x2d, wr, br, wg, wu, wd, g, b)


SORTED_ROW_WIDTH = D_MODEL + LANES
DMA_ISSUE_UNROLL = 8


def _moe_route_kernel(x_ref, wr_ref, br_ref, gate_ref, info_ref, cnt_ref, cnt_sc, lower_sc):
    i = pl.program_id(0)
    tm = x_ref.shape[0]

    @pl.when(i == 0)
    def _():
        cnt_sc[...] = jnp.zeros(cnt_sc.shape, F32)
        r = lax.broadcasted_iota(jnp.int32, (tm, tm), 0)
        c = lax.broadcasted_iota(jnp.int32, (tm, tm), 1)
        lower_sc[...] = jnp.where(c < r, 1.0, 0.0).astype(lower_sc.dtype)

    gate, gbest = _route(_router_logits(x_ref[...], wr_ref), br_ref[...])
    gate_ref[...] = gate
    lane = lax.broadcasted_iota(jnp.int32, (tm, LANES), 1)
    member = lane.astype(F32) == gbest
    onehot = jnp.where(member, 1.0, 0.0)
    before = _mm(lower_sc[...], onehot.astype(MXU_DTYPE)) + cnt_sc[0:1, :]
    rank = jnp.sum(jnp.where(member, before, 0.0), axis=1, keepdims=True)
    info_ref[...] = jnp.where(lane == 0, rank, jnp.where(lane == 1, gbest, 0.0)).astype(jnp.int32)
    cnt_sc[...] = cnt_sc[...] + jnp.sum(onehot, axis=0, keepdims=True)

    @pl.when(i == pl.num_programs(0) - 1)
    def _():
        cnt_ref[...] = cnt_sc[...]


def _moe_route(x2d, wr, br, tm):
    n = x2d.shape[0]
    return pl.pallas_call(
        _moe_route_kernel,
        grid=(n // tm,),
        in_specs=[pl.BlockSpec((tm, D_MODEL), lambda i: (i, 0)),
                  pl.BlockSpec(wr.shape, lambda i: (0, 0, 0)),
                  pl.BlockSpec(br.shape, lambda i: (0, 0))],
        out_specs=[pl.BlockSpec((tm, LANES), lambda i: (i, 0)),
                   pl.BlockSpec((tm, LANES), lambda i: (i, 0)),
                   pl.BlockSpec((SUBLANES, LANES), lambda i: (0, 0))],
        out_shape=[jax.ShapeDtypeStruct((n, LANES), F32),
                   jax.ShapeDtypeStruct((n, LANES), jnp.int32),
                   jax.ShapeDtypeStruct((SUBLANES, LANES), F32)],
        scratch_shapes=[pltpu.VMEM((SUBLANES, LANES), F32), pltpu.VMEM((tm, tm), MXU_DTYPE)],
        compiler_params=_cparams("arbitrary"),
        name="moe_route",
    )(x2d, wr, br)


def _copy_rows(n_rows, row_copy):
    def issue(n, carry):
        row_copy(n).start()
        return carry

    def wait(n, carry):
        row_copy(n).wait()
        return carry

    lax.fori_loop(0, n_rows, issue, 0, unroll=DMA_ISSUE_UNROLL)
    lax.fori_loop(0, n_rows, wait, 0, unroll=DMA_ISSUE_UNROLL)


def _moe_scatter_kernel(dest_ref, fill_ref, x_ref, gate_ref, out_ref, row_sc, sem):
    i = pl.program_id(0)
    tm = x_ref.shape[0]

    @pl.when(i == 0)
    def _():
        row_sc[...] = jnp.zeros(row_sc.shape, F32)
        for t in range(fill_ref.shape[0]):
            fill = pltpu.make_async_copy(row_sc, out_ref.at[pl.ds(fill_ref[t] * tm, tm)], sem)
            fill.start()
            fill.wait()

    row_sc[:, :D_MODEL] = x_ref[...]
    row_sc[:, D_MODEL:] = gate_ref[...]
    _copy_rows(tm, lambda n: pltpu.make_async_copy(
        row_sc.at[pl.ds(n, 1)], out_ref.at[pl.ds(dest_ref[i * tm + n], 1)], sem))


def _moe_scatter(dest, fill_tiles, x2d, gate, n_rows, tm):
    n = x2d.shape[0]
    grid_spec = pltpu.PrefetchScalarGridSpec(
        num_scalar_prefetch=2,
        grid=(n // tm,),
        in_specs=[pl.BlockSpec((tm, D_MODEL), lambda i, d, f: (i, 0)),
                  pl.BlockSpec((tm, LANES), lambda i, d, f: (i, 0))],
        out_specs=pl.BlockSpec(memory_space=pl.ANY),
        scratch_shapes=[pltpu.VMEM((tm, SORTED_ROW_WIDTH), F32), pltpu.SemaphoreType.DMA(())],
    )
    return pl.pallas_call(
        _moe_scatter_kernel,
        grid_spec=grid_spec,
        out_shape=jax.ShapeDtypeStruct((n_rows, SORTED_ROW_WIDTH), F32),
        compiler_params=_cparams("arbitrary"),
        name="moe_scatter",
    )(dest, fill_tiles, x2d, gate)


def _moe_expert_kernel(grp_ref, in_blk_ref, used_ref, rows_ref, wg_ref, wu_ref, wd_ref, y_ref):
    del in_blk_ref
    j = pl.program_id(0)

    @pl.when(used_ref[j] > 0)
    def _():
        xb = rows_ref[:, :D_MODEL].astype(MXU_DTYPE)
        y_ref[...] = _expert_ffn(xb, rows_ref[:, D_MODEL:], EXPERTS_PER_GROUP * grp_ref[j],
                                 wg_ref, wu_ref, wd_ref, EXPERTS_PER_GROUP)

    @pl.when(used_ref[j] == 0)
    def _():
        y_ref[...] = jnp.zeros(y_ref.shape, F32)


def _moe_experts(grp, in_blk, used, rows, wg, wu, wd, tmb):
    n_tiles = grp.shape[0]
    epg = EXPERTS_PER_GROUP
    grid_spec = pltpu.PrefetchScalarGridSpec(
        num_scalar_prefetch=3,
        grid=(n_tiles,),
        in_specs=[pl.BlockSpec((tmb, SORTED_ROW_WIDTH), lambda j, g, bi, u: (bi[j], 0)),
                  pl.BlockSpec((epg, D_MODEL, EXPERT_DIM), lambda j, g, bi, u: (g[j], 0, 0)),
                  pl.BlockSpec((epg, D_MODEL, EXPERT_DIM), lambda j, g, bi, u: (g[j], 0, 0)),
                  pl.BlockSpec((epg, EXPERT_DIM, D_MODEL), lambda j, g, bi, u: (g[j], 0, 0))],
        out_specs=pl.BlockSpec((tmb, D_MODEL), lambda j, g, bi, u: (j, 0)),
    )
    return pl.pallas_call(
        _moe_expert_kernel,
        grid_spec=grid_spec,
        out_shape=jax.ShapeDtypeStruct((n_tiles * tmb, D_MODEL), F32),
        compiler_params=_cparams("arbitrary"),
        name="moe_experts",
    )(grp, in_blk, used, rows, wg, wu, wd)


def _moe_gather_ln_kernel(dest_ref, x_ref, ys_ref, g_ref, b_ref, out_ref, y_sc, sem):
    i = pl.program_id(0)
    tm = x_ref.shape[0]
    _copy_rows(tm, lambda n: pltpu.make_async_copy(
        ys_ref.at[pl.ds(dest_ref[i * tm + n], 1)], y_sc.at[pl.ds(n, 1)], sem))
    out_ref[...] = _layer_norm(DEEPNORM_ALPHA * x_ref[...] + y_sc[...], g_ref[...], b_ref[...])


def _moe_gather_ln(dest, x2d, ys, g, b, tm):
    n = x2d.shape[0]
    grid_spec = pltpu.PrefetchScalarGridSpec(
        num_scalar_prefetch=1,
        grid=(n // tm,),
        in_specs=[pl.BlockSpec((tm, D_MODEL), lambda i, d: (i, 0)),
                  pl.BlockSpec(memory_space=pl.ANY),
                  pl.BlockSpec((1, D_MODEL), lambda i, d: (0, 0)),
                  pl.BlockSpec((1, D_MODEL), lambda i, d: (0, 0))],
        out_specs=pl.BlockSpec((tm, D_MODEL), lambda i, d: (i, 0)),
        scratch_shapes=[pltpu.VMEM((tm, D_MODEL), F32), pltpu.SemaphoreType.DMA(())],
    )
    return pl.pallas_call(
        _moe_gather_ln_kernel,
        grid_spec=grid_spec,
        out_shape=jax.ShapeDtypeStruct((n, D_MODEL), F32),
        compiler_params=_cparams("arbitrary"),
        name="moe_gather_ln",
    )(dest, x2d, ys, g, b)


def _moe_sorted_ln(x2d, wr, br, wg, wu, wd, g, b, tm):
    n = x2d.shape[0]
    n_tiles = n // tm + N_GROUPS - 1
    gate, info, cnt = _moe_route(x2d, wr, br, tm)
    cnt = cnt[0, :N_GROUPS].astype(jnp.int32)
    tiles_g = (cnt + tm - 1) // tm
    tile_end = jnp.cumsum(tiles_g)
    dest = ((tile_end - tiles_g) * tm)[info[:, 1]] + info[:, 0]
    n_used = tile_end[-1]
    tile_id = jnp.arange(n_tiles, dtype=jnp.int32)
    used = (tile_id < n_used).astype(jnp.int32)
    in_blk = jnp.minimum(tile_id, n_used - 1)
    grp = jnp.sum((in_blk[:, None] >= tile_end[None, :]).astype(jnp.int32), axis=1)
    fill_tiles = jnp.concatenate([jnp.maximum(tile_end - 1, 0),
                                  jnp.minimum(n_used + jnp.arange(N_GROUPS - 1, dtype=jnp.int32), n_tiles - 1)])
    rows = _moe_scatter(dest, fill_tiles, x2d, gate, n_tiles * tm, tm)
    ys = _moe_experts(grp, in_blk, used, rows, wg, wu, wd, tm)
    return _moe_gather_ln(dest, x2d, ys, g, b, tm)


def _attn_weights(w_in):
    w = ATTN_WIDTH
    qi_w = IDX_HEADS * IDX_DIM
    wq, wk, wv = w_in[:, :w], w_in[:, w:2 * w], w_in[:, 2 * w:3 * w]
    wqi = w_in[:, 3 * w:3 * w + qi_w]
    wki = w_in[:, 3 * w + qi_w:3 * w + qi_w + IDX_DIM]
    wwi = w_in[:, 3 * w + qi_w + IDX_DIM:]
    pad = jnp.zeros((D_MODEL, LANES - IDX_HEADS), w_in.dtype)
    widx = jnp.concatenate([wqi, wki, wki, wwi, pad], axis=1)
    return [a.astype(MXU_DTYPE) for a in (wq, wk, wv, widx)]


def _attn_groups(q_dtype):
    half = ATTN_HEAD_DIM // 2
    qi_w = IDX_HEADS * IDX_DIM
    qk_scale = ATTN_HEAD_DIM ** -0.5
    idx_scale = IDX_DIM ** -0.5
    return (
        (half, ATTN_WIDTH, ((0, ATTN_WIDTH, qk_scale, q_dtype),)),
        (half, ATTN_WIDTH, ((0, ATTN_WIDTH, 1.0, F32), (0, ATTN_WIDTH, 1.0, MXU_DTYPE))),
        (0, 0, ((0, ATTN_WIDTH, 1.0, F32), (0, ATTN_WIDTH, 1.0, MXU_DTYPE))),
        (half, qi_w + LANES, ((0, qi_w, idx_scale, q_dtype),
                              (qi_w, IDX_DIM, 1.0, F32),
                              (qi_w, LANES, 1.0, MXU_DTYPE),
                              (qi_w + LANES, LANES, IDX_HEADS ** -0.5, F32))),
    )


def _ret_weights(w_in):
    hq, hv = RET_HEADS * RET_QK_DIM, RET_HEADS * RET_V_DIM
    parts = (w_in[:, :hq], w_in[:, hq:2 * hq], w_in[:, 2 * hq:2 * hq + hv], w_in[:, 2 * hq + hv:])
    return [a.astype(MXU_DTYPE) for a in parts]


def _ret_groups():
    half = RET_QK_DIM // 2
    hq, hv = RET_HEADS * RET_QK_DIM, RET_HEADS * RET_V_DIM
    return (
        (half, hq, ((0, hq, 1.0, F32),)),
        (half, hq, ((0, hq, RET_QK_DIM ** -0.5, F32),)),
        (0, 0, ((0, hv, 1.0, F32),)),
        (0, 0, ((0, hv, 1.0, F32),)),
    )


def kernel(x_prompt, x_sample, cache_k, cache_v, cache_idx_k, state_ret, page_table, w_in_attn, w_out_attn, w_in_ret, ret_norm_gain, w_out_ret, w_router, b_router, w_exp_gate, w_exp_up, w_exp_down, ln_gain, ln_bias):
    batch, t_len, _ = x_prompt.shape
    dec_batch, ts, _ = x_sample.shape
    n_pages = page_table.shape[1]
    past = n_pages * PAGE_SIZE
    pool = cache_k.shape[1]
    n_p, n_s = batch * t_len, dec_batch * ts
    topk_p = min(IDX_TOPK_MAX, t_len // 4)
    topk_s = min(IDX_TOPK_MAX, (past + ts) // 4)
    tm_p = _token_tile(t_len, 512)
    tm_s = _token_tile(n_s, 512)
    pos_p = jnp.arange(t_len, dtype=jnp.int32)
    pos_s = past + (jnp.arange(tm_s, dtype=jnp.int32) % ts)

    xp = x_prompt.reshape(n_p, D_MODEL)
    xs = x_sample.reshape(n_s, D_MODEL)

    wr = jnp.concatenate([w_router, jnp.zeros((D_MODEL, LANES - N_EXPERTS), F32)], axis=1)
    wr_hi = wr.astype(MXU_DTYPE)
    wr = jnp.stack([wr_hi, (wr - wr_hi.astype(F32)).astype(MXU_DTYPE)])
    br = jnp.concatenate([b_router, jnp.zeros((LANES - N_EXPERTS,), F32)])[None, :]

    def moe(x2d, i, tm, impl):
        return impl(x2d, wr, br, w_exp_gate[i].astype(MXU_DTYPE), w_exp_up[i].astype(MXU_DTYPE),
                    w_exp_down[i].astype(MXU_DTYPE), ln_gain[i, 1][None, :], ln_bias[i, 1][None, :], tm)

    a = 0
    aw = _attn_weights(w_in_attn[a])
    w_out = w_out_attn[a].astype(MXU_DTYPE)
    g0, b0 = ln_gain[0, 0][None, :], ln_bias[0, 0][None, :]

    q_p, qi_p, wi_p, kt_p, ktb_p, vt_p, vtb_p, kit_p, kktb_p = _project_prompt_attn(
        xp, w_in_attn[a], pos_p, batch, t_len, tm_p)
    bias_p = _prompt_index_mask(qi_p, wi_p, kktb_p, batch, t_len, topk_p)
    o_p = _prompt_attention(q_p, bias_p, ktb_p, vtb_p, batch, t_len)
    xp = _outproj_ln(o_p, xp, w_out, g0, b0, tm_p)

    cos_s, sin_s = _rope_tables(pos_s, ATTN_HEAD_DIM, ATTN_WIDTH)
    q_s, k_s, _, v_s, _, qi_s, ki_s, _, wi_s = _project(
        xs, aw, _attn_groups(F32), cos_s, sin_s, 1, tm_s)
    cache_ikt = jnp.transpose(cache_idx_k[a], (0, 2, 1))
    scores_s = _sample_scores(page_table, qi_s, wi_s, ki_s, cache_ikt, dec_batch, ts)
    bias_s = _sample_mask(scores_s.reshape(n_s, past + LANES), topk_s).reshape(dec_batch, ts, past + LANES)
    cache_kt = jnp.transpose(cache_k[a], (0, 2, 3, 1)).reshape(pool, ATTN_WIDTH, PAGE_SIZE)
    cache_vt = jnp.transpose(cache_v[a], (0, 2, 3, 1)).reshape(pool, ATTN_WIDTH, PAGE_SIZE)
    o_s = _sample_attention(page_table, q_s, bias_s, k_s, v_s, cache_kt, cache_vt, dec_batch, ts)
    xs = _outproj_ln(o_s, xs, w_out, g0, b0, tm_s)

    new_k_p = jnp.transpose(kt_p, (0, 3, 1, 2))[None]
    new_v_p = jnp.transpose(vt_p, (0, 3, 1, 2))[None]
    new_ik_p = jnp.transpose(kit_p, (0, 2, 1))[None]
    new_k_s = k_s.reshape(1, dec_batch, ts, ATTN_HEADS, ATTN_HEAD_DIM)
    new_v_s = v_s.reshape(1, dec_batch, ts, ATTN_HEADS, ATTN_HEAD_DIM)
    new_ik_s = ki_s.reshape(1, dec_batch, ts, IDX_DIM)

    xp = moe(xp, 0, _token_tile(n_p, 1024), _moe_sorted_ln)
    xs = moe(xs, 0, _token_tile(n_s, 1024), _moe_ln)

    r = 0
    rw = _ret_weights(w_in_ret[r])
    w_out_r = w_out_ret[r].astype(MXU_DTYPE)
    gain = ret_norm_gain[r][None, :]
    g1, b1 = ln_gain[1, 0][None, :], ln_bias[1, 0][None, :]
    hq = RET_HEADS * RET_QK_DIM
    tm_rp = _token_tile(t_len, 256)
    tm_rs = _token_tile(n_s, 256)

    cos_p, sin_p = _rope_tables(pos_p, RET_QK_DIM, hq)
    rq, rk, rv, rg = _project(xp, rw, _ret_groups(), cos_p, sin_p, t_len // tm_rp, tm_rp)
    state0 = jnp.zeros((batch, RET_HEADS, RET_QK_DIM, RET_V_DIM), F32)
    ro, state_p = _retention_scan(rq, rk, rv, state0, batch, t_len // RET_CHUNK, RET_CHUNK)
    xp = _retention_output_ln(ro, rg, gain, xp, w_out_r, g1, b1, tm_rp)

    cos_s, sin_s = _rope_tables(past + (jnp.arange(tm_rs, dtype=jnp.int32) % ts), RET_QK_DIM, hq)
    rq, rk, rv, rg = _project(xs, rw, _ret_groups(), cos_s, sin_s, 1, tm_rs)
    ro, state_s = _retention_scan(rq, rk, rv, state_ret[r], dec_batch, 1, ts)
    xs = _retention_output_ln(ro, rg, gain, xs, w_out_r, g1, b1, tm_rs)

    xp = moe(xp, 1, _token_tile(n_p, 1024), _moe_sorted_ln)
    xs = moe(xs, 1, _token_tile(n_s, 1024), _moe_ln)

    return (xp.reshape(batch, t_len, D_MODEL), xs.reshape(dec_batch, ts, D_MODEL),
            new_k_p, new_v_p, new_ik_p, new_k_s, new_v_s, new_ik_s,
            state_p[None], state_s[None])
```

```python
import functools

import jax
import jax.numpy as jnp
from jax import lax
from jax.experimental import pallas as pl
from jax.experimental.pallas import tpu as pltpu

F32 = jnp.float32
BF16 = jnp.bfloat16
MXU_DTYPE = jnp.bfloat16

D_MODEL = 1024
PAGE_SIZE = 128
ATTN_HEADS = 16
ATTN_HEAD_DIM = 64
ATTN_WIDTH = ATTN_HEADS * ATTN_HEAD_DIM
IDX_HEADS = 8
IDX_DIM = 64
IDX_TOPK_MAX = 256
Q_BLOCK = 128
ROPE_THETA = 10000.0
RET_HEADS = 4
RET_QK_DIM = D_MODEL // RET_HEADS
RET_V_DIM = 2 * D_MODEL // RET_HEADS
RET_CHUNK = 128
N_EXPERTS = 16
N_GROUPS = 4
EXPERTS_PER_GROUP = N_EXPERTS // N_GROUPS
EXPERT_DIM = D_MODEL // 2
DEPTH = 2
DEEPNORM_ALPHA = (2 * DEPTH) ** 0.25
LN_EPS = 1e-5

LANES = 128
SUBLANES = 8
VMEM_LIMIT_BYTES = 58 * 1024 * 1024

NEG_INF = float("-inf")


def _cparams(*sem):
    return pltpu.CompilerParams(dimension_semantics=sem, vmem_limit_bytes=VMEM_LIMIT_BYTES)


def _mm(a, b):
    return jnp.dot(a, b, preferred_element_type=F32)


def _mm_nt(a, b):
    return lax.dot_general(a, b, (((1,), (1,)), ((), ())), preferred_element_type=F32)


def _mm_tn(a, b):
    return lax.dot_general(a, b, (((0,), (0,)), ((), ())), preferred_element_type=F32)


def _layer_norm(z, g, b):
    mu = jnp.mean(z, axis=-1, keepdims=True)
    d = z - mu
    var = jnp.mean(d * d, axis=-1, keepdims=True)
    return d * lax.rsqrt(var + LN_EPS) * g + b


def _sigmoid(x):
    return 1.0 / (1.0 + jnp.exp(-x))


def _rope_tables(pos, head_dim, width):
    half = head_dim // 2
    inv = ROPE_THETA ** (-jnp.arange(0, head_dim, 2, dtype=F32) / head_dim)
    ang = pos.astype(F32)[:, None] * inv[None, :]
    cos, sin = jnp.cos(ang), jnp.sin(ang)
    cos_h = jnp.concatenate([cos, cos], axis=1)
    sin_h = jnp.concatenate([-sin, sin], axis=1)
    reps = width // head_dim
    return jnp.tile(cos_h, (1, reps)), jnp.tile(sin_h, (1, reps))


def _rope(y, cos, sin, half):
    w = y.shape[-1]
    lane = lax.broadcasted_iota(jnp.int32, y.shape, 1)
    first = (lane % (2 * half)) < half
    swapped = jnp.where(first, pltpu.roll(y, w - half, 1), pltpu.roll(y, half, 1))
    return y * cos + swapped * sin


def _proj_kernel(x_ref, cos_ref, sin_ref, *refs, groups):
    n_g = len(groups)
    w_refs, out_refs = refs[:n_g], refs[n_g:]
    xb = x_ref[...].astype(MXU_DTYPE)
    oi = 0
    for g, (rope_half, rope_w, outs) in enumerate(groups):
        y = _mm(xb, w_refs[g][...])
        if rope_half:
            yr = _rope(y[:, :rope_w], cos_ref[:, :rope_w], sin_ref[:, :rope_w], rope_half)
            y = yr if rope_w == y.shape[1] else jnp.concatenate([yr, y[:, rope_w:]], axis=1)
        for (c0, w, scale, dt) in outs:
            piece = y[:, c0:c0 + w]
            if scale != 1.0:
                piece = piece * scale
            out_refs[oi][...] = piece.astype(dt)
            oi += 1


def _token_tile(n, cap):
    tm = min(cap, n)
    assert n % tm == 0 and tm % SUBLANES == 0
    return tm


def _project(x2d, weights, groups, cos, sin, n_tab_blocks, tm):
    n = x2d.shape[0]
    out_shapes, out_specs = [], []
    for (_, _, outs) in groups:
        for (_, w, _, dt) in outs:
            out_shapes.append(jax.ShapeDtypeStruct((n, w), dt))
            out_specs.append(pl.BlockSpec((tm, w), lambda m: (m, 0)))
    tw = cos.shape[1]
    in_specs = [pl.BlockSpec((tm, D_MODEL), lambda m: (m, 0)),
                pl.BlockSpec((tm, tw), lambda m: (m % n_tab_blocks, 0)),
                pl.BlockSpec((tm, tw), lambda m: (m % n_tab_blocks, 0))]
    for w in weights:
        in_specs.append(pl.BlockSpec(w.shape, lambda m: (0, 0)))
    return pl.pallas_call(
        functools.partial(_proj_kernel, groups=groups),
        grid=(n // tm,),
        in_specs=in_specs,
        out_specs=out_specs,
        out_shape=out_shapes,
        compiler_params=_cparams("parallel"),
        name="proj",
    )(x2d, cos, sin, *weights)


LOG2_E = 1.4426950408889634


def _rope_t(yt, cos_t, sin_t, head_dim):
    rows, m = yt.shape
    half = head_dim // 2
    y3 = yt.reshape(rows // head_dim, head_dim, m)
    x1, x2 = y3[:, :half, :], y3[:, half:, :]
    out = jnp.concatenate([x1 * cos_t - x2 * sin_t, x1 * sin_t + x2 * cos_t], axis=1)
    return out.reshape(rows, m)


def _proj_prompt_attn_kernel(x_ref, cos_ref, sin_ref, cos_t_ref, sin_t_ref,
                             wq_ref, widx_ref, wkt_ref, wvt_ref, wkit_ref,
                             q_ref, qi_ref, wi_ref, kt_ref, ktb_ref, vt_ref, vtb_ref, kit_ref, kktb_ref):
    xb = x_ref[...].astype(MXU_DTYPE)
    half = ATTN_HEAD_DIM // 2
    qi_w = IDX_HEADS * IDX_DIM
    q = _rope(_mm(xb, wq_ref[...]), cos_ref[...], sin_ref[...], half)
    q_ref[...] = (q * (ATTN_HEAD_DIM ** -0.5 * LOG2_E)).astype(q_ref.dtype)
    yi = _mm(xb, widx_ref[...])
    qi = _rope(yi[:, :qi_w], cos_ref[:, :qi_w], sin_ref[:, :qi_w], half)
    qi_ref[...] = (qi * IDX_DIM ** -0.5).astype(qi_ref.dtype)
    wi_ref[...] = yi[:, qi_w:] * IDX_HEADS ** -0.5
    cos_t, sin_t = cos_t_ref[...], sin_t_ref[...]
    kt = _rope_t(_mm_nt(wkt_ref[...], xb), cos_t, sin_t, ATTN_HEAD_DIM)
    kt_ref[...] = kt.reshape(kt_ref.shape)
    ktb_ref[...] = kt.astype(ktb_ref.dtype)
    vt = _mm_nt(wvt_ref[...], xb)
    vt_ref[...] = vt.reshape(vt_ref.shape)
    vtb_ref[...] = vt.astype(vtb_ref.dtype)
    kit = _rope_t(_mm_nt(wkit_ref[...], xb), cos_t, sin_t, IDX_DIM)
    kit_ref[...] = kit
    kib = kit.astype(kktb_ref.dtype)
    kktb_ref[...] = jnp.concatenate([kib, kib], axis=0)


def _project_prompt_attn(x2d, w_in, pos, batch, t_len, tm):
    n = x2d.shape[0]
    nt = t_len // tm
    w = ATTN_WIDTH
    qi_w = IDX_HEADS * IDX_DIM
    cos, sin = _rope_tables(pos, ATTN_HEAD_DIM, w)
    half = ATTN_HEAD_DIM // 2
    cos_t, sin_t = jnp.transpose(cos[:, :half]), jnp.transpose(sin[:, half:ATTN_HEAD_DIM])
    pad = jnp.zeros((D_MODEL, LANES - IDX_HEADS), w_in.dtype)
    weights = [w_in[:, :w],
               jnp.concatenate([w_in[:, 3 * w:3 * w + qi_w], w_in[:, 3 * w + qi_w + IDX_DIM:], pad], axis=1),
               jnp.transpose(w_in[:, w:2 * w]),
               jnp.transpose(w_in[:, 2 * w:3 * w]),
               jnp.transpose(w_in[:, 3 * w + qi_w:3 * w + qi_w + IDX_DIM])]
    weights = [a.astype(MXU_DTYPE) for a in weights]
    heads_t = (batch, ATTN_HEADS, ATTN_HEAD_DIM, t_len)
    out_shape = [jax.ShapeDtypeStruct((n, w), MXU_DTYPE),
                 jax.ShapeDtypeStruct((n, qi_w), MXU_DTYPE),
                 jax.ShapeDtypeStruct((n, LANES), F32),
                 jax.ShapeDtypeStruct(heads_t, F32),
                 jax.ShapeDtypeStruct((batch, w, t_len), MXU_DTYPE),
                 jax.ShapeDtypeStruct(heads_t, F32),
                 jax.ShapeDtypeStruct((batch, w, t_len), MXU_DTYPE),
                 jax.ShapeDtypeStruct((batch, IDX_DIM, t_len), F32),
                 jax.ShapeDtypeStruct((batch, 2 * IDX_DIM, t_len), MXU_DTYPE)]

    def rows(width):
        return pl.BlockSpec((tm, width), lambda m: (m, 0))

    def cols(height):
        return pl.BlockSpec((None, height, tm), lambda m: (m // nt, 0, m % nt))

    heads_spec = pl.BlockSpec((None, ATTN_HEADS, ATTN_HEAD_DIM, tm), lambda m: (m // nt, 0, 0, m % nt))
    out_specs = [rows(w), rows(qi_w), rows(LANES), heads_spec, cols(w), heads_spec, cols(w),
                 cols(IDX_DIM), cols(2 * IDX_DIM)]
    in_specs = [rows(D_MODEL),
                pl.BlockSpec((tm, w), lambda m: (m % nt, 0)),
                pl.BlockSpec((tm, w), lambda m: (m % nt, 0)),
                pl.BlockSpec((half, tm), lambda m: (0, m % nt)),
                pl.BlockSpec((half, tm), lambda m: (0, m % nt))]
    in_specs += [pl.BlockSpec(a.shape, lambda m: (0, 0)) for a in weights]
    return pl.pallas_call(
        _proj_prompt_attn_kernel,
        grid=(n // tm,),
        in_specs=in_specs,
        out_specs=out_specs,
        out_shape=out_shape,
        compiler_params=_cparams("parallel"),
        name="proj_prompt_attn",
    )(x2d, cos, sin, cos_t, sin_t, *weights)


def _row_count(pred):
    return jnp.sum(jnp.where(pred, 1.0, 0.0), axis=1, keepdims=True)


KEY_NEG_INF = -2139095041


def _key_to_float(key):
    bits = key ^ ((key >> 31) & jnp.int32(0x7FFFFFFF))
    return jnp.where(key <= KEY_NEG_INF, NEG_INF, lax.bitcast_convert_type(bits, F32))


def _topk_select(score, k):
    r, s = score.shape
    kf = jnp.float32(k)

    def value_bit(i, carry):
        t, cnt = carry
        cand = t + lax.shift_left(jnp.int32(1), jnp.int32(31) - i)
        c = _row_count(score >= _key_to_float(cand))
        ok = c >= kf
        return jnp.where(ok, cand, t), jnp.where(ok, c, cnt)

    t, c_ge = lax.fori_loop(0, 32, value_bit,
                            (jnp.full((r, 1), jnp.iinfo(jnp.int32).min, jnp.int32),
                             jnp.full((r, 1), float(s), F32)))
    def lowest(mask):
        v = jnp.min(jnp.where(mask, score, jnp.inf), axis=1, keepdims=True)
        return v, _row_count(score == v)

    thr, n_eq = lowest(score >= _key_to_float(t))

    def step_up(carry):
        thr, n_eq, c_ge = carry
        for _ in range(3):
            more = (c_ge - n_eq) >= kf
            above, n_above = lowest(score > thr)
            thr = jnp.where(more, above, thr)
            c_ge = jnp.where(more, c_ge - n_eq, c_ge)
            n_eq = jnp.where(more, n_above, n_eq)
        return thr, n_eq, c_ge

    thr, n_eq, c_ge = lax.cond(jnp.max(c_ge - n_eq) >= kf, step_up, lambda carry: carry, (thr, n_eq, c_ge))
    gt = score > thr
    eq = score == thr
    need = kf - (c_ge - n_eq)
    col = lax.broadcasted_iota(jnp.int32, (r, s), 1)
    n_bits = max(1, (s - 1).bit_length())

    def last_tie_column():
        def index_bit(i, j):
            cand = j + lax.shift_left(jnp.int32(1), jnp.int32(n_bits - 1) - i)
            return jnp.where(_row_count(eq & (col < cand)) < need, cand, j)
        return lax.fori_loop(0, n_bits, index_bit, jnp.zeros((r, 1), jnp.int32))

    j = lax.cond(jnp.max(n_eq - need) > 0.0, last_tie_column, lambda: jnp.full((r, 1), s, jnp.int32))
    keep = gt | (eq & (col <= j))
    return keep & (score > NEG_INF)


def _causal_widths(t_len):
    step = min(2 * Q_BLOCK, t_len)
    assert t_len % step == 0
    return tuple(range(step, t_len + 1, step))


def _for_causal_width(n_keys, widths, run, enable=True):
    prev = 0
    for w in widths:
        pl.when(enable & (n_keys > prev) & (n_keys <= w))(functools.partial(run, w))
        prev = w


def _idx_mask_kernel(qi_ref, wi_ref, kk_ref, bias_ref, *, topk, widths):
    i = pl.program_id(1)
    qb, t_len = qi_ref.shape[0], kk_ref.shape[1]
    n_keys = (i + 1) * qb
    few = n_keys <= topk

    @pl.when(few)
    def _():
        row = i * qb + lax.broadcasted_iota(jnp.int32, (qb, t_len), 0)
        col = lax.broadcasted_iota(jnp.int32, (qb, t_len), 1)
        bias_ref[...] = jnp.where(col <= row, 0.0, NEG_INF).astype(bias_ref.dtype)

    def run(w):
        qi = qi_ref[...]
        kk = kk_ref[:, :w]
        wi = wi_ref[...]
        lane = lax.broadcasted_iota(jnp.int32, (qb, LANES), 1)
        zero = jnp.zeros((qb, LANES), qi.dtype)
        score = jnp.zeros((qb, w), F32)
        for p in range(IDX_HEADS // 2):
            qp = qi[:, LANES * p:LANES * (p + 1)]
            for half in range(2):
                lhs = jnp.where(lane < IDX_DIM, qp, zero) if half == 0 else jnp.where(lane >= IDX_DIM, qp, zero)
                h = 2 * p + half
                score = score + jnp.maximum(_mm(lhs, kk), 0.0) * wi[:, h:h + 1]
        row = i * qb + lax.broadcasted_iota(jnp.int32, (qb, w), 0)
        col = lax.broadcasted_iota(jnp.int32, (qb, w), 1)
        score = jnp.where(col <= row, score, NEG_INF)
        keep = _topk_select(score, topk)
        bias_ref[:, :w] = jnp.where(keep, 0.0, NEG_INF).astype(bias_ref.dtype)
        if w < t_len:
            bias_ref[:, w:] = jnp.full((qb, t_len - w), NEG_INF, bias_ref.dtype)

    _for_causal_width(n_keys, widths, run, enable=jnp.logical_not(few))


def _prompt_index_mask(qi, wi, kk, batch, t_len, topk):
    nb = t_len // Q_BLOCK
    return pl.pallas_call(
        functools.partial(_idx_mask_kernel, topk=topk, widths=_causal_widths(t_len)),
        grid=(batch, nb),
        in_specs=[pl.BlockSpec((Q_BLOCK, qi.shape[1]), lambda b, i: (b * nb + i, 0)),
                  pl.BlockSpec((Q_BLOCK, LANES), lambda b, i: (b * nb + i, 0)),
                  pl.BlockSpec((None, LANES, t_len), lambda b, i: (b, 0, 0))],
        out_specs=pl.BlockSpec((Q_BLOCK, t_len), lambda b, i: (b * nb + i, 0)),
        out_shape=jax.ShapeDtypeStruct((batch * t_len, t_len), BF16),
        compiler_params=_cparams("parallel", "arbitrary"),
        name="prompt_index_mask",
    )(qi, wi, kk)


def _attn_kernel(q_ref, bias_ref, k_ref, v_ref, o_ref, *, widths):
    i = pl.program_id(1)
    qb = q_ref.shape[0]

    def run(w):
        bias = bias_ref[:, :w].astype(F32)
        bias2 = jnp.concatenate([bias, bias], axis=0)
        lane = lax.broadcasted_iota(jnp.int32, (qb, LANES), 1)
        low = lane < ATTN_HEAD_DIM
        zero = jnp.zeros((qb, LANES), q_ref.dtype)
        for p in range(ATTN_HEADS // 2):
            sl = slice(LANES * p, LANES * (p + 1))
            qp = q_ref[:, sl]
            lhs = jnp.concatenate([jnp.where(low, qp, zero), jnp.where(low, zero, qp)], axis=0)
            s = _mm(lhs, k_ref[sl, :w]) + bias2
            m = jnp.max(s, axis=1, keepdims=True)
            e = jnp.exp2(s - m)
            l = jnp.sum(e, axis=1, keepdims=True)
            pv = _mm_nt(e.astype(MXU_DTYPE), v_ref[sl, :w]) / l
            o_ref[:, sl] = jnp.where(low, pv[:qb], pv[qb:]).astype(o_ref.dtype)

    _for_causal_width((i + 1) * qb, widths, run)


def _prompt_attention(q, bias, k, v, batch, t_len):
    nb = t_len // Q_BLOCK
    return pl.pallas_call(
        functools.partial(_attn_kernel, widths=_causal_widths(t_len)),
        grid=(batch, nb),
        in_specs=[pl.BlockSpec((Q_BLOCK, ATTN_WIDTH), lambda b, i: (b * nb + i, 0)),
                  pl.BlockSpec((Q_BLOCK, t_len), lambda b, i: (b * nb + i, 0)),
                  pl.BlockSpec((None, ATTN_WIDTH, t_len), lambda b, i: (b, 0, 0)),
                  pl.BlockSpec((None, ATTN_WIDTH, t_len), lambda b, i: (b, 0, 0))],
        out_specs=pl.BlockSpec((Q_BLOCK, ATTN_WIDTH), lambda b, i: (b * nb + i, 0)),
        out_shape=jax.ShapeDtypeStruct((batch * t_len, ATTN_WIDTH), MXU_DTYPE),
        compiler_params=_cparams("parallel", "arbitrary"),
        name="prompt_attention",
    )(q, bias, k, v)


def _sample_score_kernel(pt_ref, qi_ref, wi_ref, kin_ref, *refs, n_pages):
    del pt_ref
    page_refs, out_ref = refs[:n_pages], refs[n_pages]
    ts = qi_ref.shape[0]
    qi = qi_ref[...]
    wi = wi_ref[...]
    lhs = jnp.concatenate([qi[:, IDX_DIM * h:IDX_DIM * (h + 1)] for h in range(IDX_HEADS)],
                          axis=0).astype(MXU_DTYPE)
    wcol = jnp.concatenate([wi[:, h:h + 1] for h in range(IDX_HEADS)], axis=0)

    def head_sum(s):
        w = jnp.maximum(s, 0.0) * wcol
        acc = w[0:ts]
        for h in range(1, IDX_HEADS):
            acc = acc + w[ts * h:ts * (h + 1)]
        return acc

    for j in range(n_pages):
        kpt = page_refs[j][0].astype(MXU_DTYPE)
        out_ref[0, :, PAGE_SIZE * j:PAGE_SIZE * (j + 1)] = head_sum(_mm(lhs, kpt))
    kn = kin_ref[...].astype(MXU_DTYPE)
    kn = jnp.concatenate([kn, jnp.zeros((LANES - ts, IDX_DIM), MXU_DTYPE)], axis=0)
    sn = head_sum(_mm_nt(lhs, kn))
    row = lax.broadcasted_iota(jnp.int32, sn.shape, 0)
    col = lax.broadcasted_iota(jnp.int32, sn.shape, 1)
    past = PAGE_SIZE * n_pages
    out_ref[0, :, past:past + LANES] = jnp.where(col <= row, sn, NEG_INF)


def _sample_scores(page_table, qi, wi, ki_new, cache_ikt, dec_batch, ts):
    n_pages = page_table.shape[1]
    past = n_pages * PAGE_SIZE

    def page_spec(j):
        return pl.BlockSpec((1, IDX_DIM, PAGE_SIZE), lambda b, pt: (pt[b, j], 0, 0))

    grid_spec = pltpu.PrefetchScalarGridSpec(
        num_scalar_prefetch=1,
        grid=(dec_batch,),
        in_specs=[pl.BlockSpec((ts, qi.shape[1]), lambda b, pt: (b, 0)),
                  pl.BlockSpec((ts, LANES), lambda b, pt: (b, 0)),
                  pl.BlockSpec((ts, IDX_DIM), lambda b, pt: (b, 0))]
                 + [page_spec(j) for j in range(n_pages)],
        out_specs=pl.BlockSpec((1, ts, past + LANES), lambda b, pt: (b, 0, 0)),
    )
    return pl.pallas_call(
        functools.partial(_sample_score_kernel, n_pages=n_pages),
        grid_spec=grid_spec,
        out_shape=jax.ShapeDtypeStruct((dec_batch, ts, past + LANES), F32),
        compiler_params=_cparams("arbitrary"),
        name="sample_scores",
    )(page_table, qi, wi, ki_new, *([cache_ikt] * n_pages))


def _mask_kernel(score_ref, bias_ref, *, topk):
    keep = _topk_select(score_ref[...], topk)
    bias_ref[...] = jnp.where(keep, 0.0, NEG_INF).astype(bias_ref.dtype)


def _sample_mask(scores2d, topk):
    n, s = scores2d.shape
    rb = _token_tile(n, Q_BLOCK)
    return pl.pallas_call(
        functools.partial(_mask_kernel, topk=topk),
        grid=(n // rb,),
        in_specs=[pl.BlockSpec((rb, s), lambda i: (i, 0))],
        out_specs=pl.BlockSpec((rb, s), lambda i: (i, 0)),
        out_shape=jax.ShapeDtypeStruct((n, s), BF16),
        compiler_params=_cparams("parallel"),
        name="sample_mask",
    )(scores2d)


def _head_lane_mask(shape, h):
    lane = lax.broadcasted_iota(jnp.int32, shape, 1)
    return (lane >= ATTN_HEAD_DIM * h) & (lane < ATTN_HEAD_DIM * (h + 1))


def _sample_attn_kernel(pt_ref, q_ref, bias_ref, kn_ref, vn_ref, *refs, n_pages):
    del pt_ref
    k_pages, v_pages, o_ref = refs[:n_pages], refs[n_pages:2 * n_pages], refs[2 * n_pages]
    ts = q_ref.shape[0]
    past = n_pages * PAGE_SIZE
    q = q_ref[...]
    qbd = jnp.concatenate([jnp.where(_head_lane_mask(q.shape, h), q, 0.0)
                           for h in range(ATTN_HEADS)], axis=0).astype(MXU_DTYPE)
    pad = jnp.zeros((LANES - ts, ATTN_WIDTH), F32)
    kn = jnp.concatenate([kn_ref[...], pad], axis=0).astype(MXU_DTYPE)
    vn = jnp.concatenate([vn_ref[...], pad], axis=0).astype(MXU_DTYPE)
    s = jnp.concatenate([_mm(qbd, k_pages[pg][0].astype(MXU_DTYPE)) for pg in range(n_pages)]
                        + [_mm_nt(qbd, kn)], axis=1)
    s = s + jnp.concatenate([bias_ref[0].astype(F32)] * ATTN_HEADS, axis=0)
    m = jnp.max(s, axis=1, keepdims=True)
    e = jnp.exp(s - m)
    l = jnp.sum(e, axis=1, keepdims=True)
    eb = e.astype(MXU_DTYPE)
    pv = _mm(eb[:, past:], vn)
    for pg in range(n_pages):
        pv = pv + _mm_nt(eb[:, PAGE_SIZE * pg:PAGE_SIZE * (pg + 1)], v_pages[pg][0].astype(MXU_DTYPE))
    out = pv / l
    o = jnp.zeros((ts, ATTN_WIDTH), F32)
    for h in range(ATTN_HEADS):
        o = o + jnp.where(_head_lane_mask(o.shape, h), out[ts * h:ts * (h + 1)], 0.0)
    o_ref[...] = o.astype(o_ref.dtype)


def _sample_attention(page_table, q, bias, k_new, v_new, cache_kt, cache_vt, dec_batch, ts):
    n_pages = page_table.shape[1]
    past = n_pages * PAGE_SIZE

    def page_spec(g):
        return pl.BlockSpec((1, ATTN_WIDTH, PAGE_SIZE), lambda b, pt: (pt[b, g], 0, 0))

    grid_spec = pltpu.PrefetchScalarGridSpec(
        num_scalar_prefetch=1,
        grid=(dec_batch,),
        in_specs=[pl.BlockSpec((ts, ATTN_WIDTH), lambda b, pt: (b, 0)),
                  pl.BlockSpec((1, ts, past + LANES), lambda b, pt: (b, 0, 0)),
                  pl.BlockSpec((ts, ATTN_WIDTH), lambda b, pt: (b, 0)),
                  pl.BlockSpec((ts, ATTN_WIDTH), lambda b, pt: (b, 0))]
                 + [page_spec(g) for g in range(n_pages)] * 2,
        out_specs=pl.BlockSpec((ts, ATTN_WIDTH), lambda b, pt: (b, 0)),
    )
    return pl.pallas_call(
        functools.partial(_sample_attn_kernel, n_pages=n_pages),
        grid_spec=grid_spec,
        out_shape=jax.ShapeDtypeStruct((dec_batch * ts, ATTN_WIDTH), MXU_DTYPE),
        compiler_params=_cparams("parallel"),
        name="sample_attention",
    )(page_table, q, bias, k_new, v_new, *([cache_kt] * n_pages), *([cache_vt] * n_pages))


def _outproj_ln_kernel(o_ref, x_ref, w_ref, g_ref, b_ref, y_ref):
    m = _mm(o_ref[...].astype(MXU_DTYPE), w_ref[...])
    y_ref[...] = _layer_norm(DEEPNORM_ALPHA * x_ref[...] + m, g_ref[...], b_ref[...])


def _outproj_ln(o, x2d, w, g, b, tm):
    n = x2d.shape[0]
    return pl.pallas_call(
        _outproj_ln_kernel,
        grid=(n // tm,),
        in_specs=[pl.BlockSpec((tm, o.shape[1]), lambda m: (m, 0)),
                  pl.BlockSpec((tm, D_MODEL), lambda m: (m, 0)),
                  pl.BlockSpec(w.shape, lambda m: (0, 0)),
                  pl.BlockSpec((1, D_MODEL), lambda m: (0, 0)),
                  pl.BlockSpec((1, D_MODEL), lambda m: (0, 0))],
        out_specs=pl.BlockSpec((tm, D_MODEL), lambda m: (m, 0)),
        out_shape=jax.ShapeDtypeStruct((n, D_MODEL), F32),
        compiler_params=_cparams("parallel"),
        name="outproj_ln",
    )(o, x2d, w, g, b)


def _retention_tables(chunk):
    lg = jnp.log(1.0 - 2.0 ** (-5.0 - jnp.arange(RET_HEADS, dtype=F32)))
    i = jnp.arange(chunk, dtype=F32)
    diff = i[:, None] - i[None, :]
    dmask = jnp.where(diff[None] >= 0, jnp.exp(jnp.maximum(diff, 0.0)[None] * lg[:, None, None]), 0.0)
    cross = jnp.exp((i + 1.0)[:, None] * lg[None, :])
    kdec = jnp.exp((chunk - 1.0 - i)[:, None] * lg[None, :])
    cdec = jnp.exp(chunk * lg)
    pad = jnp.zeros((chunk, LANES - RET_HEADS), F32)
    cross = jnp.concatenate([cross, pad], axis=1)
    kdec = jnp.concatenate([kdec, pad], axis=1)
    cdec = jnp.broadcast_to(cdec[:, None, None], (RET_HEADS, 1, RET_V_DIM))
    return dmask.astype(F32), cross, kdec, cdec


def _retention_kernel(q_ref, k_ref, v_ref, s0_ref, dmask_ref, cross_ref, kdec_ref, cdec_ref,
                      o_ref, s_out_ref, state_sc):
    c = pl.program_id(1)

    @pl.when(c == 0)
    def _():
        state_sc[...] = s0_ref[0]

    cross = cross_ref[...]
    kdec = kdec_ref[...]
    for h in range(RET_HEADS):
        qs = slice(RET_QK_DIM * h, RET_QK_DIM * (h + 1))
        vs = slice(RET_V_DIM * h, RET_V_DIM * (h + 1))
        qh = q_ref[:, qs]
        kh = k_ref[:, qs]
        qb = qh.astype(MXU_DTYPE)
        vb = v_ref[:, vs].astype(MXU_DTYPE)
        state = state_sc[h]
        inner = _mm_nt(qb, kh.astype(MXU_DTYPE)) * dmask_ref[h]
        o = _mm(inner.astype(MXU_DTYPE), vb) + _mm(qb, state.astype(MXU_DTYPE)) * cross[:, h:h + 1]
        o_ref[:, vs] = o
        kd = (kh * kdec[:, h:h + 1]).astype(MXU_DTYPE)
        state_sc[h] = cdec_ref[h] * state + _mm_tn(kd, vb)

    @pl.when(c == pl.num_programs(1) - 1)
    def _():
        s_out_ref[0] = state_sc[...]


def _retention_scan(q, k, v, state0, batch, n_chunks, chunk):
    dmask, cross, kdec, cdec = _retention_tables(chunk)
    hq, hv = RET_HEADS * RET_QK_DIM, RET_HEADS * RET_V_DIM
    state_block = (1, RET_HEADS, RET_QK_DIM, RET_V_DIM)
    return pl.pallas_call(
        _retention_kernel,
        grid=(batch, n_chunks),
        in_specs=[pl.BlockSpec((chunk, hq), lambda b, c: (b * n_chunks + c, 0)),
                  pl.BlockSpec((chunk, hq), lambda b, c: (b * n_chunks + c, 0)),
                  pl.BlockSpec((chunk, hv), lambda b, c: (b * n_chunks + c, 0)),
                  pl.BlockSpec(state_block, lambda b, c: (b, 0, 0, 0)),
                  pl.BlockSpec(dmask.shape, lambda b, c: (0, 0, 0)),
                  pl.BlockSpec(cross.shape, lambda b, c: (0, 0)),
                  pl.BlockSpec(kdec.shape, lambda b, c: (0, 0)),
                  pl.BlockSpec(cdec.shape, lambda b, c: (0, 0, 0))],
        out_specs=[pl.BlockSpec((chunk, hv), lambda b, c: (b * n_chunks + c, 0)),
                   pl.BlockSpec(state_block, lambda b, c: (b, 0, 0, 0))],
        out_shape=[jax.ShapeDtypeStruct((batch * n_chunks * chunk, hv), F32),
                   jax.ShapeDtypeStruct((batch,) + state_block[1:], F32)],
        scratch_shapes=[pltpu.VMEM(state_block[1:], F32)],
        compiler_params=_cparams("parallel", "arbitrary"),
        name="retention_scan",
    )(q, k, v, state0, dmask, cross, kdec, cdec)


def _ret_out_kernel(o_ref, gate_ref, gain_ref, x_ref, w_ref, g_ref, b_ref, y_ref):
    pieces = []
    for h in range(RET_HEADS):
        vs = slice(RET_V_DIM * h, RET_V_DIM * (h + 1))
        of = o_ref[:, vs]
        mu = jnp.mean(of, axis=-1, keepdims=True)
        d = of - mu
        var = jnp.mean(d * d, axis=-1, keepdims=True)
        on = d * lax.rsqrt(var + LN_EPS) * gain_ref[:, vs]
        gt = gate_ref[:, vs]
        pieces.append(((gt * _sigmoid(gt)) * on).astype(MXU_DTYPE))
    m = _mm(jnp.concatenate(pieces, axis=1), w_ref[...])
    y_ref[...] = _layer_norm(DEEPNORM_ALPHA * x_ref[...] + m, g_ref[...], b_ref[...])


def _retention_output_ln(o, gate, gain, x2d, w, g, b, tm):
    n = x2d.shape[0]
    hv = RET_HEADS * RET_V_DIM
    return pl.pallas_call(
        _ret_out_kernel,
        grid=(n // tm,),
        in_specs=[pl.BlockSpec((tm, hv), lambda m: (m, 0)),
                  pl.BlockSpec((tm, hv), lambda m: (m, 0)),
                  pl.BlockSpec((1, hv), lambda m: (0, 0)),
                  pl.BlockSpec((tm, D_MODEL), lambda m: (m, 0)),
                  pl.BlockSpec(w.shape, lambda m: (0, 0)),
                  pl.BlockSpec((1, D_MODEL), lambda m: (0, 0)),
                  pl.BlockSpec((1, D_MODEL), lambda m: (0, 0))],
        out_specs=pl.BlockSpec((tm, D_MODEL), lambda m: (m, 0)),
        out_shape=jax.ShapeDtypeStruct((n, D_MODEL), F32),
        compiler_params=_cparams("parallel"),
        name="retention_out_ln",
    )(o, gate, gain, x2d, w, g, b)


def _route(logits, b_router):
    shape = logits.shape
    lane = lax.broadcasted_iota(jnp.int32, shape, 1)
    lane_f = lane.astype(F32)
    valid = lane < N_EXPERTS
    lg = jnp.where(valid, logits, NEG_INF)
    ex = jnp.exp(lg - jnp.max(lg, axis=1, keepdims=True))
    probs = ex / jnp.sum(ex, axis=1, keepdims=True)
    sel = probs + b_router
    far = jnp.float32(LANES)

    def first_max(v):
        m = jnp.max(v, axis=1, keepdims=True)
        idx = jnp.min(jnp.where(v == m, lane_f, far), axis=1, keepdims=True)
        return m, idx

    gscore = []
    for g in range(N_GROUPS):
        in_g = (lane >= EXPERTS_PER_GROUP * g) & (lane < EXPERTS_PER_GROUP * (g + 1))
        sg = jnp.where(in_g, sel, NEG_INF)
        m1, i1 = first_max(sg)
        m2 = jnp.max(jnp.where(lane_f == i1, NEG_INF, sg), axis=1, keepdims=True)
        gscore.append(m1 + m2)
    best, gbest = gscore[0], jnp.zeros_like(gscore[0])
    for g in range(1, N_GROUPS):
        better = gscore[g] > best
        best = jnp.where(better, gscore[g], best)
        gbest = jnp.where(better, jnp.float32(g), gbest)
    lo = gbest * EXPERTS_PER_GROUP
    in_best = (lane_f >= lo) & (lane_f < lo + EXPERTS_PER_GROUP)
    sm = jnp.where(in_best, sel, NEG_INF)
    _, e1 = first_max(sm)
    _, e2 = first_max(jnp.where(lane_f == e1, NEG_INF, sm))
    is1, is2 = lane_f == e1, lane_f == e2
    p1 = jnp.sum(jnp.where(is1, probs, 0.0), axis=1, keepdims=True)
    p2 = jnp.sum(jnp.where(is2, probs, 0.0), axis=1, keepdims=True)
    den = p1 + p2
    return jnp.where(is1, p1 / den, 0.0) + jnp.where(is2, p2 / den, 0.0), gbest


def _router_logits(x, wr_ref):
    xh = x.astype(MXU_DTYPE)
    xl = (x - xh.astype(F32)).astype(MXU_DTYPE)
    return _mm(xh, wr_ref[0]) + (_mm(xh, wr_ref[1]) + _mm(xl, wr_ref[0]))


def _expert_ffn(xb, gate, first_expert, wg_ref, wu_ref, wd_ref, n_experts):
    lane = lax.broadcasted_iota(jnp.int32, gate.shape, 1)
    acc = None
    for e in range(n_experts):
        ge = jnp.sum(jnp.where(lane == first_expert + e, gate, 0.0), axis=1, keepdims=True)
        hg = _mm(xb, wg_ref[e])
        hu = _mm(xb, wu_ref[e])
        h = (hg * _sigmoid(hg)) * hu * ge
        y = _mm(h.astype(MXU_DTYPE), wd_ref[e])
        acc = y if acc is None else acc + y
    return acc


def _moe_kernel(x_ref, wr_ref, br_ref, wg_ref, wu_ref, wd_ref, g_ref, b_ref, y_ref,
                xb_sc, gate_sc, acc_sc):
    e = pl.program_id(1)

    @pl.when(e == 0)
    def _():
        x = x_ref[...]
        xb_sc[...] = x.astype(MXU_DTYPE)
        gate_sc[...] = _route(_router_logits(x, wr_ref), br_ref[...])[0]
        acc_sc[...] = jnp.zeros(acc_sc.shape, F32)

    acc_sc[...] += _expert_ffn(xb_sc[...], gate_sc[...], e, wg_ref, wu_ref, wd_ref, 1)

    @pl.when(e == pl.num_programs(1) - 1)
    def _():
        y_ref[...] = _layer_norm(DEEPNORM_ALPHA * x_ref[...] + acc_sc[...], g_ref[...], b_ref[...])


def _moe_ln(x2d, wr, br, wg, wu, wd, g, b, tm):
    n = x2d.shape[0]
    return pl.pallas_call(
        _moe_kernel,
        grid=(n // tm, N_EXPERTS),
        in_specs=[pl.BlockSpec((tm, D_MODEL), lambda m, e: (m, 0)),
                  pl.BlockSpec(wr.shape, lambda m, e: (0, 0, 0)),
                  pl.BlockSpec(br.shape, lambda m, e: (0, 0)),
                  pl.BlockSpec((1, D_MODEL, EXPERT_DIM), lambda m, e: (e, 0, 0)),
                  pl.BlockSpec((1, D_MODEL, EXPERT_DIM), lambda m, e: (e, 0, 0)),
                  pl.BlockSpec((1, EXPERT_DIM, D_MODEL), lambda m, e: (e, 0, 0)),
                  pl.BlockSpec((1, D_MODEL), lambda m, e: (0, 0)),
                  pl.BlockSpec((1, D_MODEL), lambda m, e: (0, 0))],
        out_specs=pl.BlockSpec((tm, D_MODEL), lambda m, e: (m, 0)),
        out_shape=jax.ShapeDtypeStruct((n, D_MODEL), F32),
        scratch_shapes=[pltpu.VMEM((tm, D_MODEL), MXU_DTYPE),
                        pltpu.VMEM((tm, LANES), F32),
                        pltpu.VMEM((tm, D_MODEL), F32)],
        compiler_params=_cparams("parallel", "arbitrary"),
        name="moe_ln",
    )(x2d, wr, br, wg, wu, wd, g, b)


SORTED_ROW_WIDTH = D_MODEL + LANES
DMA_ISSUE_UNROLL = 8


def _moe_route_kernel(x_ref, wr_ref, br_ref, gate_ref, info_ref, cnt_ref, cnt_sc, lower_sc):
    i = pl.program_id(0)
    tm = x_ref.shape[0]

    @pl.when(i == 0)
    def _():
        cnt_sc[...] = jnp.zeros(cnt_sc.shape, F32)
        r = lax.broadcasted_iota(jnp.int32, (tm, tm), 0)
        c = lax.broadcasted_iota(jnp.int32, (tm, tm), 1)
        lower_sc[...] = jnp.where(c < r, 1.0, 0.0).astype(lower_sc.dtype)

    gate, gbest = _route(_router_logits(x_ref[...], wr_ref), br_ref[...])
    gate_ref[...] = gate
    lane = lax.broadcasted_iota(jnp.int32, (tm, LANES), 1)
    member = lane.astype(F32) == gbest
    onehot = jnp.where(member, 1.0, 0.0)
    before = _mm(lower_sc[...], onehot.astype(MXU_DTYPE)) + cnt_sc[0:1, :]
    rank = jnp.sum(jnp.where(member, before, 0.0), axis=1, keepdims=True)
    info_ref[...] = jnp.where(lane == 0, rank, jnp.where(lane == 1, gbest, 0.0)).astype(jnp.int32)
    cnt_sc[...] = cnt_sc[...] + jnp.sum(onehot, axis=0, keepdims=True)

    @pl.when(i == pl.num_programs(0) - 1)
    def _():
        cnt_ref[...] = cnt_sc[...]


def _moe_route(x2d, wr, br, tm):
    n = x2d.shape[0]
    return pl.pallas_call(
        _moe_route_kernel,
        grid=(n // tm,),
        in_specs=[pl.BlockSpec((tm, D_MODEL), lambda i: (i, 0)),
                  pl.BlockSpec(wr.shape, lambda i: (0, 0, 0)),
                  pl.BlockSpec(br.shape, lambda i: (0, 0))],
        out_specs=[pl.BlockSpec((tm, LANES), lambda i: (i, 0)),
                   pl.BlockSpec((tm, LANES), lambda i: (i, 0)),
                   pl.BlockSpec((SUBLANES, LANES), lambda i: (0, 0))],
        out_shape=[jax.ShapeDtypeStruct((n, LANES), F32),
                   jax.ShapeDtypeStruct((n, LANES), jnp.int32),
                   jax.ShapeDtypeStruct((SUBLANES, LANES), F32)],
        scratch_shapes=[pltpu.VMEM((SUBLANES, LANES), F32), pltpu.VMEM((tm, tm), MXU_DTYPE)],
        compiler_params=_cparams("arbitrary"),
        name="moe_route",
    )(x2d, wr, br)


def _copy_rows(n_rows, row_copy):
    def issue(n, carry):
        row_copy(n).start()
        return carry

    def wait(n, carry):
        row_copy(n).wait()
        return carry

    lax.fori_loop(0, n_rows, issue, 0, unroll=DMA_ISSUE_UNROLL)
    lax.fori_loop(0, n_rows, wait, 0, unroll=DMA_ISSUE_UNROLL)


def _moe_scatter_kernel(dest_ref, fill_ref, x_ref, gate_ref, out_ref, row_sc, sem):
    i = pl.program_id(0)
    tm = x_ref.shape[0]

    @pl.when(i == 0)
    def _():
        row_sc[...] = jnp.zeros(row_sc.shape, F32)
        for t in range(fill_ref.shape[0]):
            fill = pltpu.make_async_copy(row_sc, out_ref.at[pl.ds(fill_ref[t] * tm, tm)], sem)
            fill.start()
            fill.wait()

    row_sc[:, :D_MODEL] = x_ref[...]
    row_sc[:, D_MODEL:] = gate_ref[...]
    _copy_rows(tm, lambda n: pltpu.make_async_copy(
        row_sc.at[pl.ds(n, 1)], out_ref.at[pl.ds(dest_ref[i * tm + n], 1)], sem))


def _moe_scatter(dest, fill_tiles, x2d, gate, n_rows, tm):
    n = x2d.shape[0]
    grid_spec = pltpu.PrefetchScalarGridSpec(
        num_scalar_prefetch=2,
        grid=(n // tm,),
        in_specs=[pl.BlockSpec((tm, D_MODEL), lambda i, d, f: (i, 0)),
                  pl.BlockSpec((tm, LANES), lambda i, d, f: (i, 0))],
        out_specs=pl.BlockSpec(memory_space=pl.ANY),
        scratch_shapes=[pltpu.VMEM((tm, SORTED_ROW_WIDTH), F32), pltpu.SemaphoreType.DMA(())],
    )
    return pl.pallas_call(
        _moe_scatter_kernel,
        grid_spec=grid_spec,
        out_shape=jax.ShapeDtypeStruct((n_rows, SORTED_ROW_WIDTH), F32),
        compiler_params=_cparams("arbitrary"),
        name="moe_scatter",
    )(dest, fill_tiles, x2d, gate)


def _moe_expert_kernel(grp_ref, in_blk_ref, used_ref, rows_ref, wg_ref, wu_ref, wd_ref, y_ref):
    del in_blk_ref
    j = pl.program_id(0)

    @pl.when(used_ref[j] > 0)
    def _():
        xb = rows_ref[:, :D_MODEL].astype(MXU_DTYPE)
        y_ref[...] = _expert_ffn(xb, rows_ref[:, D_MODEL:], EXPERTS_PER_GROUP * grp_ref[j],
                                 wg_ref, wu_ref, wd_ref, EXPERTS_PER_GROUP)

    @pl.when(used_ref[j] == 0)
    def _():
        y_ref[...] = jnp.zeros(y_ref.shape, F32)


def _moe_experts(grp, in_blk, used, rows, wg, wu, wd, tmb):
    n_tiles = grp.shape[0]
    epg = EXPERTS_PER_GROUP
    grid_spec = pltpu.PrefetchScalarGridSpec(
        num_scalar_prefetch=3,
        grid=(n_tiles,),
        in_specs=[pl.BlockSpec((tmb, SORTED_ROW_WIDTH), lambda j, g, bi, u: (bi[j], 0)),
                  pl.BlockSpec((epg, D_MODEL, EXPERT_DIM), lambda j, g, bi, u: (g[j], 0, 0)),
                  pl.BlockSpec((epg, D_MODEL, EXPERT_DIM), lambda j, g, bi, u: (g[j], 0, 0)),
                  pl.BlockSpec((epg, EXPERT_DIM, D_MODEL), lambda j, g, bi, u: (g[j], 0, 0))],
        out_specs=pl.BlockSpec((tmb, D_MODEL), lambda j, g, bi, u: (j, 0)),
    )
    return pl.pallas_call(
        _moe_expert_kernel,
        grid_spec=grid_spec,
        out_shape=jax.ShapeDtypeStruct((n_tiles * tmb, D_MODEL), F32),
        compiler_params=_cparams("arbitrary"),
        name="moe_experts",
    )(grp, in_blk, used, rows, wg, wu, wd)


def _moe_gather_ln_kernel(dest_ref, x_ref, ys_ref, g_ref, b_ref, out_ref, y_sc, sem):
    i = pl.program_id(0)
    tm = x_ref.shape[0]
    _copy_rows(tm, lambda n: pltpu.make_async_copy(
        ys_ref.at[pl.ds(dest_ref[i * tm + n], 1)], y_sc.at[pl.ds(n, 1)], sem))
    out_ref[...] = _layer_norm(DEEPNORM_ALPHA * x_ref[...] + y_sc[...], g_ref[...], b_ref[...])


def _moe_gather_ln(dest, x2d, ys, g, b, tm):
    n = x2d.shape[0]
    grid_spec = pltpu.PrefetchScalarGridSpec(
        num_scalar_prefetch=1,
        grid=(n // tm,),
        in_specs=[pl.BlockSpec((tm, D_MODEL), lambda i, d: (i, 0)),
                  pl.BlockSpec(memory_space=pl.ANY),
                  pl.BlockSpec((1, D_MODEL), lambda i, d: (0, 0)),
                  pl.BlockSpec((1, D_MODEL), lambda i, d: (0, 0))],
        out_specs=pl.BlockSpec((tm, D_MODEL), lambda i, d: (i, 0)),
        scratch_shapes=[pltpu.VMEM((tm, D_MODEL), F32), pltpu.SemaphoreType.DMA(())],
    )
    return pl.pallas_call(
        _moe_gather_ln_kernel,
        grid_spec=grid_spec,
        out_shape=jax.ShapeDtypeStruct((n, D_MODEL), F32),
        compiler_params=_cparams("arbitrary"),
        name="moe_gather_ln",
    )(dest, x2d, ys, g, b)


def _moe_sorted_ln(x2d, wr, br, wg, wu, wd, g, b, tm):
    n = x2d.shape[0]
    n_tiles = n // tm + N_GROUPS - 1
    gate, info, cnt = _moe_route(x2d, wr, br, tm)
    cnt = cnt[0, :N_GROUPS].astype(jnp.int32)
    tiles_g = (cnt + tm - 1) // tm
    tile_end = jnp.cumsum(tiles_g)
    dest = ((tile_end - tiles_g) * tm)[info[:, 1]] + info[:, 0]
    n_used = tile_end[-1]
    tile_id = jnp.arange(n_tiles, dtype=jnp.int32)
    used = (tile_id < n_used).astype(jnp.int32)
    in_blk = jnp.minimum(tile_id, n_used - 1)
    grp = jnp.sum((in_blk[:, None] >= tile_end[None, :]).astype(jnp.int32), axis=1)
    fill_tiles = jnp.concatenate([jnp.maximum(tile_end - 1, 0),
                                  jnp.minimum(n_used + jnp.arange(N_GROUPS - 1, dtype=jnp.int32), n_tiles - 1)])
    rows = _moe_scatter(dest, fill_tiles, x2d, gate, n_tiles * tm, tm)
    ys = _moe_experts(grp, in_blk, used, rows, wg, wu, wd, tm)
    return _moe_gather_ln(dest, x2d, ys, g, b, tm)


def _attn_weights(w_in):
    w = ATTN_WIDTH
    qi_w = IDX_HEADS * IDX_DIM
    wq, wk, wv = w_in[:, :w], w_in[:, w:2 * w], w_in[:, 2 * w:3 * w]
    wqi = w_in[:, 3 * w:3 * w + qi_w]
    wki = w_in[:, 3 * w + qi_w:3 * w + qi_w + IDX_DIM]
    wwi = w_in[:, 3 * w + qi_w + IDX_DIM:]
    pad = jnp.zeros((D_MODEL, LANES - IDX_HEADS), w_in.dtype)
    widx = jnp.concatenate([wqi, wki, wki, wwi, pad], axis=1)
    return [a.astype(MXU_DTYPE) for a in (wq, wk, wv, widx)]


def _attn_groups(q_dtype):
    half = ATTN_HEAD_DIM // 2
    qi_w = IDX_HEADS * IDX_DIM
    qk_scale = ATTN_HEAD_DIM ** -0.5
    idx_scale = IDX_DIM ** -0.5
    return (
        (half, ATTN_WIDTH, ((0, ATTN_WIDTH, qk_scale, q_dtype),)),
        (half, ATTN_WIDTH, ((0, ATTN_WIDTH, 1.0, F32), (0, ATTN_WIDTH, 1.0, MXU_DTYPE))),
        (0, 0, ((0, ATTN_WIDTH, 1.0, F32), (0, ATTN_WIDTH, 1.0, MXU_DTYPE))),
        (half, qi_w + LANES, ((0, qi_w, idx_scale, q_dtype),
                              (qi_w, IDX_DIM, 1.0, F32),
                              (qi_w, LANES, 1.0, MXU_DTYPE),
                              (qi_w + LANES, LANES, IDX_HEADS ** -0.5, F32))),
    )


def _ret_weights(w_in):
    hq, hv = RET_HEADS * RET_QK_DIM, RET_HEADS * RET_V_DIM
    parts = (w_in[:, :hq], w_in[:, hq:2 * hq], w_in[:, 2 * hq:2 * hq + hv], w_in[:, 2 * hq + hv:])
    return [a.astype(MXU_DTYPE) for a in parts]


def _ret_groups():
    half = RET_QK_DIM // 2
    hq, hv = RET_HEADS * RET_QK_DIM, RET_HEADS * RET_V_DIM
    return (
        (half, hq, ((0, hq, 1.0, F32),)),
        (half, hq, ((0, hq, RET_QK_DIM ** -0.5, F32),)),
        (0, 0, ((0, hv, 1.0, F32),)),
        (0, 0, ((0, hv, 1.0, F32),)),
    )


def kernel(x_prompt, x_sample, cache_k, cache_v, cache_idx_k, state_ret, page_table, w_in_attn, w_out_attn, w_in_ret, ret_norm_gain, w_out_ret, w_router, b_router, w_exp_gate, w_exp_up, w_exp_down, ln_gain, ln_bias):
    batch, t_len, _ = x_prompt.shape
    dec_batch, ts, _ = x_sample.shape
    n_pages = page_table.shape[1]
    past = n_pages * PAGE_SIZE
    pool = cache_k.shape[1]
    n_p, n_s = batch * t_len, dec_batch * ts
    topk_p = min(IDX_TOPK_MAX, t_len // 4)
    topk_s = min(IDX_TOPK_MAX, (past + ts) // 4)
    tm_p = _token_tile(t_len, 512)
    tm_s = _token_tile(n_s, 512)
    pos_p = jnp.arange(t_len, dtype=jnp.int32)
    pos_s = past + (jnp.arange(tm_s, dtype=jnp.int32) % ts)

    xp = x_prompt.reshape(n_p, D_MODEL)
    xs = x_sample.reshape(n_s, D_MODEL)

    wr = jnp.concatenate([w_router, jnp.zeros((D_MODEL, LANES - N_EXPERTS), F32)], axis=1)
    wr_hi = wr.astype(MXU_DTYPE)
    wr = jnp.stack([wr_hi, (wr - wr_hi.astype(F32)).astype(MXU_DTYPE)])
    br = jnp.concatenate([b_router, jnp.zeros((LANES - N_EXPERTS,), F32)])[None, :]

    def moe(x2d, i, tm, impl):
        return impl(x2d, wr, br, w_exp_gate[i].astype(MXU_DTYPE), w_exp_up[i].astype(MXU_DTYPE),
                    w_exp_down[i].astype(MXU_DTYPE), ln_gain[i, 1][None, :], ln_bias[i, 1][None, :], tm)

    a = 0
    aw = _attn_weights(w_in_attn[a])
    w_out = w_out_attn[a].astype(MXU_DTYPE)
    g0, b0 = ln_gain[0, 0][None, :], ln_bias[0, 0][None, :]

    q_p, qi_p, wi_p, kt_p, ktb_p, vt_p, vtb_p, kit_p, kktb_p = _project_prompt_attn(
        xp, w_in_attn[a], pos_p, batch, t_len, tm_p)
    bias_p = _prompt_index_mask(qi_p, wi_p, kktb_p, batch, t_len, topk_p)
    o_p = _prompt_attention(q_p, bias_p, ktb_p, vtb_p, batch, t_len)
    xp = _outproj_ln(o_p, xp, w_out, g0, b0, tm_p)

    cos_s, sin_s = _rope_tables(pos_s, ATTN_HEAD_DIM, ATTN_WIDTH)
    q_s, k_s, _, v_s, _, qi_s, ki_s, _, wi_s = _project(
        xs, aw, _attn_groups(F32), cos_s, sin_s, 1, tm_s)
    cache_ikt = jnp.transpose(cache_idx_k[a], (0, 2, 1))
    scores_s = _sample_scores(page_table, qi_s, wi_s, ki_s, cache_ikt, dec_batch, ts)
    bias_s = _sample_mask(scores_s.reshape(n_s, past + LANES), topk_s).reshape(dec_batch, ts, past + LANES)
    cache_kt = jnp.transpose(cache_k[a], (0, 2, 3, 1)).reshape(pool, ATTN_WIDTH, PAGE_SIZE)
    cache_vt = jnp.transpose(cache_v[a], (0, 2, 3, 1)).reshape(pool, ATTN_WIDTH, PAGE_SIZE)
    o_s = _sample_attention(page_table, q_s, bias_s, k_s, v_s, cache_kt, cache_vt, dec_batch, ts)
    xs = _outproj_ln(o_s, xs, w_out, g0, b0, tm_s)

    new_k_p = jnp.transpose(kt_p, (0, 3, 1, 2))[None]
    new_v_p = jnp.transpose(vt_p, (0, 3, 1, 2))[None]
    new_ik_p = jnp.transpose(kit_p, (0, 2, 1))[None]
    new_k_s = k_s.reshape(1, dec_batch, ts, ATTN_HEADS, ATTN_HEAD_DIM)
    new_v_s = v_s.reshape(1, dec_batch, ts, ATTN_HEADS, ATTN_HEAD_DIM)
    new_ik_s = ki_s.reshape(1, dec_batch, ts, IDX_DIM)

    xp = moe(xp, 0, _token_tile(n_p, 1024), _moe_sorted_ln)
    xs = moe(xs, 0, _token_tile(n_s, 256), _moe_sorted_ln)

    r = 0
    rw = _ret_weights(w_in_ret[r])
    w_out_r = w_out_ret[r].astype(MXU_DTYPE)
    gain = ret_norm_gain[r][None, :]
    g1, b1 = ln_gain[1, 0][None, :], ln_bias[1, 0][None, :]
    hq = RET_HEADS * RET_QK_DIM
    tm_rp = _token_tile(t_len, 256)
    tm_rs = _token_tile(n_s, 256)

    cos_p, sin_p = _rope_tables(pos_p, RET_QK_DIM, hq)
    rq, rk, rv, rg = _project(xp, rw, _ret_groups(), cos_p, sin_p, t_len // tm_rp, tm_rp)
    state0 = jnp.zeros((batch, RET_HEADS, RET_QK_DIM, RET_V_DIM), F32)
    ro, state_p = _retention_scan(rq, rk, rv, state0, batch, t_len // RET_CHUNK, RET_CHUNK)
    xp = _retention_output_ln(ro, rg, gain, xp, w_out_r, g1, b1, tm_rp)

    cos_s, sin_s = _rope_tables(past + (jnp.arange(tm_rs, dtype=jnp.int32) % ts), RET_QK_DIM, hq)
    rq, rk, rv, rg = _project(xs, rw, _ret_groups(), cos_s, sin_s, 1, tm_rs)
    ro, state_s = _retention_scan(rq, rk, rv, state_ret[r], dec_batch, 1, ts)
    xs = _retention_output_ln(ro, rg, gain, xs, w_out_r, g1, b1, tm_rs)

    xp = moe(xp, 1, _token_tile(n_p, 1024), _moe_sorted_ln)
    xs = moe(xs, 1, _token_tile(n_s, 256), _moe_sorted_ln)

    return (xp.reshape(batch, t_len, D_MODEL), xs.reshape(dec_batch, ts, D_MODEL),
            new_k_p, new_v_p, new_ik_p, new_k_s, new_v_s, new_ik_s,
            state_p[None], state_s[None])
```
